```python
import math
import jax, jax.numpy as jnp
from jax import lax
import numpy as np

D_MODEL = 2048
BATCH = 4
SEQ = 4096
DEPTH = 4

DN_HEADS = 16
DN_DK = 128
DN_DV = 128
DN_CONV = 4
DN_CHUNK = 64
DN_QK = DN_HEADS * DN_DK
DN_V = DN_HEADS * DN_DV
DA_HEADS = 16
DA_HEAD_DIM = 64
DA_V_DIM = 2 * DA_HEAD_DIM
DA_QK = DA_HEADS * 2 * DA_HEAD_DIM
DA_V = DA_HEADS * DA_V_DIM
Q_BLOCK = 128
D_FF = -(-8 * D_MODEL // (3 * 256)) * 256
DEEPNORM_ALPHA = (2 * DEPTH) ** 0.25
DEEPNORM_BETA = (8 * DEPTH) ** -0.25
LN_EPS = 1e-5
RMS_EPS = 1e-6
IN_SIZES = (DN_QK, DN_QK, DN_V, DN_V, DN_HEADS, DN_HEADS, DA_QK, DA_QK, DA_V, D_MODEL, D_MODEL)
D_IN = sum(IN_SIZES)

kernel_name = 'hybrid_gdn_diffattn_alibi_deepnorm'


def _partition_bounds():
    bounds = [0]
    for s in IN_SIZES:
        bounds.append(bounds[-1] + s)
    return bounds


def _alibi_slopes(n_heads):
    h = jnp.arange(1, n_heads + 1, dtype=jnp.float32)
    return jnp.exp2(-8.0 * h / n_heads)


def layer_norm(x, g, b):
    xf = x.astype(jnp.float32)
    mu = jnp.mean(xf, axis=-1, keepdims=True)
    xc = xf - mu
    var = jnp.mean(xc * xc, axis=-1, keepdims=True)
    return (xc * lax.rsqrt(var + LN_EPS) * g + b).astype(x.dtype)


def rms_norm(x, w):
    xf = x.astype(jnp.float32)
    return xf * lax.rsqrt(jnp.mean(xf * xf, axis=-1, keepdims=True) + RMS_EPS) * w


def l2_normalize(x):
    xf = x.astype(jnp.float32)
    return xf * lax.rsqrt(jnp.sum(xf * xf, axis=-1, keepdims=True) + RMS_EPS)


def causal_depthwise_conv(x, w):
    K, C = w.shape
    return lax.conv_general_dilated(
        x, w[:, None, :].astype(x.dtype), window_strides=(1,), padding=[(K - 1, 0)],
        dimension_numbers=('NWC', 'WIO', 'NWC'), feature_group_count=C)


def chunk_gated_delta_rule(q, k, v, g, beta):
    B, H, L, dk = q.shape
    dv = v.shape[-1]
    C = DN_CHUNK
    N = L // C
    f32 = jnp.float32
    q = q.astype(f32).reshape(B, H, N, C, dk)
    k = k.astype(f32).reshape(B, H, N, C, dk)
    v = v.astype(f32).reshape(B, H, N, C, dv)
    g = jnp.cumsum(g.astype(f32).reshape(B, H, N, C), axis=-1)
    beta = beta.astype(f32).reshape(B, H, N, C, 1)
    causal = jnp.tril(jnp.ones((C, C), dtype=bool))
    strict = jnp.tril(jnp.ones((C, C), dtype=bool), -1)
    decay = jnp.exp(jnp.where(causal, g[..., :, None] - g[..., None, :], -jnp.inf))
    k_beta = k * beta
    a_mat = jnp.where(strict, jnp.einsum('bhncd,bhnsd->bhncs', k_beta, k) * decay, 0.0)
    rhs = jnp.concatenate([v * beta, k_beta * jnp.exp(g)[..., None]], axis=-1)
    sol = lax.linalg.triangular_solve(a_mat + jnp.eye(C, dtype=f32), rhs, left_side=True,
                                      lower=True, unit_diagonal=True)
    u, w = sol[..., :dv], sol[..., dv:]
    qk = jnp.einsum('bhncd,bhnsd->bhncs', q, k) * decay
    q_dec = q * jnp.exp(g)[..., None]
    k_tail = k * jnp.exp(g[..., -1:] - g)[..., None]
    g_tot = jnp.exp(g[..., -1])

    def step(S, xs):
        qk_c, qd_c, u_c, w_c, kt_c, gt_c = xs
        v_new = u_c - jnp.einsum('bhcd,bhde->bhce', w_c, S)
        o_c = jnp.einsum('bhcd,bhde->bhce', qd_c, S) + jnp.einsum('bhcs,bhse->bhce', qk_c, v_new)
        S = S * gt_c[..., None, None] + jnp.einsum('bhcd,bhce->bhde', kt_c, v_new)
        return S, o_c

    xs = (jnp.moveaxis(qk, 2, 0), jnp.moveaxis(q_dec, 2, 0), jnp.moveaxis(u, 2, 0),
          jnp.moveaxis(w, 2, 0), jnp.moveaxis(k_tail, 2, 0), jnp.moveaxis(g_tot, 2, 0))
    S0 = jnp.zeros((B, H, dk, dv), f32)
    _, o = lax.scan(step, S0, xs)
    return jnp.moveaxis(o, 0, 2).reshape(B, H, L, dv)


def gated_deltanet(q, k, v, z, beta_logit, a_logit, conv_w, a_log, dt_bias, norm_w):
    B, L, _ = q.shape
    qkv = jax.nn.silu(causal_depthwise_conv(jnp.concatenate([q, k, v], axis=-1), conv_w))
    q, k, v = jnp.split(qkv, [DN_QK, 2 * DN_QK], axis=-1)
    q = l2_normalize(q.reshape(B, L, DN_HEADS, DN_DK)) * (DN_DK ** -0.5)
    k = l2_normalize(k.reshape(B, L, DN_HEADS, DN_DK))
    v = v.reshape(B, L, DN_HEADS, DN_DV)
    beta = jax.nn.sigmoid(beta_logit.astype(jnp.float32))
    g = -jnp.exp(a_log) * jax.nn.softplus(a_logit.astype(jnp.float32) + dt_bias)
    o = chunk_gated_delta_rule(q.transpose(0, 2, 1, 3), k.transpose(0, 2, 1, 3), v.transpose(0, 2, 1, 3),
                               g.transpose(0, 2, 1), beta.transpose(0, 2, 1))
    o = o.transpose(0, 2, 1, 3)
    o = rms_norm(o, norm_w) * jax.nn.silu(z.reshape(B, L, DN_HEADS, DN_DV).astype(jnp.float32))
    return o.reshape(B, L, DN_V).astype(z.dtype)


def diff_attention(q, k, v, lam_params, norm_w, layer_idx):
    B, L, _ = q.shape
    q = q.reshape(B, L, DA_HEADS, 2, DA_HEAD_DIM).transpose(0, 2, 3, 1, 4)
    k = k.reshape(B, L, DA_HEADS, 2, DA_HEAD_DIM).transpose(0, 2, 3, 1, 4)
    v = v.reshape(B, L, DA_HEADS, DA_V_DIM).transpose(0, 2, 1, 3)
    lambda_init = 0.8 - 0.6 * math.exp(-0.3 * layer_idx)
    lp = lam_params.astype(jnp.float32)
    lam = jnp.exp(jnp.sum(lp[0] * lp[1])) - jnp.exp(jnp.sum(lp[2] * lp[3])) + lambda_init
    slopes = _alibi_slopes(DA_HEADS)
    scale = DA_HEAD_DIM ** -0.5
    outs = []
    for blk in range(L // Q_BLOCK):
        start, end = blk * Q_BLOCK, (blk + 1) * Q_BLOCK
        qb = q[:, :, :, start:end]
        kb = k[:, :, :, :end]
        vb = v[:, :, :end]
        dist = (jnp.arange(start, end)[:, None] - jnp.arange(end)[None, :]).astype(jnp.float32)
        bias = jnp.where(dist >= 0, -slopes[:, None, None] * dist, -jnp.inf)
        s = jnp.einsum('bhmqd,bhmkd->bhmqk', qb, kb).astype(jnp.float32) * scale + bias[None, :, None]
        p = jax.nn.softmax(s, axis=-1)
        p_diff = p[:, :, 0] - lam * p[:, :, 1]
        outs.append(jnp.einsum('bhqk,bhkd->bhqd', p_diff, vb.astype(jnp.float32)))
    o = jnp.concatenate(outs, axis=2)
    o = rms_norm(o, norm_w) * (1.0 - lambda_init)
    return o.transpose(0, 2, 1, 3).reshape(B, L, DA_V).astype(v.dtype)


def setup_inputs(seed: int = 0) -> dict:
    key = jax.random.key(seed)
    ks = jax.random.split(key, 16)
    f32 = jnp.float32
    bounds = _partition_bounds()

    def dense(k, shape, fan_in, gain=1.0):
        return jax.random.normal(k, shape, f32) * (gain * fan_in ** -0.5)

    col_scale = np.ones((D_IN,), np.float32)
    col_scale[bounds[2]:bounds[3]] = DEEPNORM_BETA
    col_scale[bounds[8]:bounds[9]] = DEEPNORM_BETA

    x = jax.random.normal(ks[0], (BATCH, SEQ, D_MODEL), f32)
    w_in = dense(ks[1], (DEPTH, D_MODEL, D_IN), D_MODEL) * jnp.asarray(col_scale)
    conv_w = dense(ks[2], (DEPTH, DN_CONV, 2 * DN_QK + DN_V), DN_CONV)
    a_log = jnp.log(jax.random.uniform(ks[3], (DEPTH, DN_HEADS), f32, 1.0, 16.0))
    dt = jnp.exp(jax.random.uniform(ks[4], (DEPTH, DN_HEADS), f32, math.log(1e-3), math.log(1e-1)))
    dt_bias = dt + jnp.log(-jnp.expm1(-dt))
    dn_norm_w = 1.0 + 0.02 * jax.random.normal(ks[5], (DEPTH, DN_DV), f32)
    da_lambda = 0.1 * jax.random.normal(ks[6], (DEPTH, 4, DA_HEAD_DIM), f32)
    da_norm_w = 1.0 + 0.02 * jax.random.normal(ks[7], (DEPTH, DA_V_DIM), f32)
    w_branch_a = dense(ks[8], (DEPTH, DN_V, D_MODEL), DN_V, DEEPNORM_BETA)
    w_branch_b = dense(ks[9], (DEPTH, DA_V, D_MODEL), DA_V, DEEPNORM_BETA)
    w_out = dense(ks[10], (DEPTH, D_MODEL, D_MODEL), D_MODEL, DEEPNORM_BETA)
    ln_g = 1.0 + 0.02 * jax.random.normal(ks[11], (DEPTH, 2, D_MODEL), f32)
    ln_b = 0.02 * jax.random.normal(ks[12], (DEPTH, 2, D_MODEL), f32)
    w_gate = dense(ks[13], (DEPTH, D_MODEL, D_FF), D_MODEL, DEEPNORM_BETA)
    w_up = dense(ks[14], (DEPTH, D_MODEL, D_FF), D_MODEL, DEEPNORM_BETA)
    w_down = dense(ks[15], (DEPTH, D_FF, D_MODEL), D_FF, DEEPNORM_BETA)
    return {'x': x, 'w_in': w_in, 'conv_w': conv_w, 'a_log': a_log, 'dt_bias': dt_bias,
            'dn_norm_w': dn_norm_w, 'da_lambda': da_lambda, 'da_norm_w': da_norm_w,
            'w_branch_a': w_branch_a, 'w_branch_b': w_branch_b, 'w_out': w_out,
            'ln_g': ln_g, 'ln_b': ln_b, 'w_gate': w_gate, 'w_up': w_up, 'w_down': w_down}


def reference(x, w_in, conv_w, a_log, dt_bias, dn_norm_w, da_lambda, da_norm_w,
              w_branch_a, w_branch_b, w_out, ln_g, ln_b, w_gate, w_up, w_down):
    split_at = _partition_bounds()[1:-1]
    for l in range(DEPTH):
        h = x @ w_in[l]
        dq, dk, dv, dz, db, da, aq, ak, av, ga, gb = jnp.split(h, split_at, axis=-1)
        o_a = gated_deltanet(dq, dk, dv, dz, db, da, conv_w[l], a_log[l], dt_bias[l], dn_norm_w[l])
        o_b = diff_attention(aq, ak, av, da_lambda[l], da_norm_w[l], l)
        merged = jax.nn.sigmoid(ga) * (o_a @ w_branch_a[l]) + jax.nn.sigmoid(gb) * (o_b @ w_branch_b[l])
        x = layer_norm(DEEPNORM_ALPHA * x + merged @ w_out[l], ln_g[l, 0], ln_b[l, 0])
        ffn = (jax.nn.silu(x @ w_gate[l]) * (x @ w_up[l])) @ w_down[l]
        x = layer_norm(DEEPNORM_ALPHA * x + ffn, ln_g[l, 1], ln_b[l, 1])
    return x
```

```python
import functools
import math

import jax
import jax.numpy as jnp
from jax import lax
from jax.experimental import pallas as pl
from jax.experimental.pallas import tpu as pltpu

_F32 = jnp.float32
_BF16 = jnp.bfloat16
_LANE = 128
_SUBLANE = 8
_VMEM_LIMIT = 56 * 1024 * 1024
_CHUNK = 64
_PAIR = 2 * _CHUNK
_CONV_TAPS = 4
_LN_EPS = 1e-5
_RMS_EPS = 1e-6
_NEG = -1e30

_NT = (((1,), (1,)), ((), ()))


def _mm(a, b):
    return jnp.dot(a, b, preferred_element_type=_F32)


def _mm_nt(a, b):
    return lax.dot_general(a, b, _NT, preferred_element_type=_F32)


def _cparams(semantics):
    return pltpu.CompilerParams(dimension_semantics=semantics, vmem_limit_bytes=_VMEM_LIMIT)


def _sigmoid(x):
    return 1.0 / (1.0 + jnp.exp(-x))


def _softplus(x):
    return jnp.maximum(x, 0.0) + jnp.log(1.0 + jnp.exp(-jnp.abs(x)))


def _tile(n, pref):
    t = min(n, pref)
    while n % t:
        t -= _LANE
    return t


def _proj_body(x_ref, w_ref, o_ref):
    acc = _mm(x_ref[...], w_ref[...])
    for j in range(o_ref.shape[0]):
        o_ref[j] = acc[:, j * _LANE:(j + 1) * _LANE].astype(o_ref.dtype)


def _project(xb, w_all, layer):
    m, k = xb.shape
    n = w_all.shape[2]
    tm, tn = _tile(m, 1024), _tile(n, 512)
    return pl.pallas_call(
        _proj_body,
        grid=(m // tm, n // tn),
        in_specs=[pl.BlockSpec((tm, k), lambda i, j: (i, 0)),
                  pl.BlockSpec((None, k, tn), lambda i, j: (layer, 0, j))],
        out_specs=pl.BlockSpec((tn // _LANE, tm, _LANE), lambda i, j: (j, i, 0)),
        out_shape=jax.ShapeDtypeStruct((n // _LANE, m, _LANE), _BF16),
        compiler_params=_cparams(("parallel", "arbitrary")),
        name="in_proj",
    )(xb, w_all)


def _rowproj_body(w_ref, x_ref, o_ref):
    o_ref[...] = _mm_nt(w_ref[...], x_ref[...])


def _project_rows(xb, wt_all, layer):
    m, k = xb.shape
    r = wt_all.shape[1]
    tm = _tile(m, 1024)
    return pl.pallas_call(
        _rowproj_body,
        grid=(m // tm,),
        in_specs=[pl.BlockSpec((None, r, k), lambda i: (layer, 0, 0)),
                  pl.BlockSpec((tm, k), lambda i: (i, 0))],
        out_specs=pl.BlockSpec((r, tm), lambda i: (0, i)),
        out_shape=jax.ShapeDtypeStruct((r, m), _F32),
        compiler_params=_cparams(("parallel",)),
        name="gate_logits",
    )(wt_all, xb)


def _dn_body(ab_ref, q_ref, k_ref, v_ref, z_ref, cq_ref, ck_ref, cv_ref, bl_ref, al_ref, nw_ref,
             o_ref, s_ref, tail_ref, xb_ref, *, hb, tl, dk):
    hg = pl.program_id(1)
    t = pl.program_id(2)
    npair = tl // _PAIR

    @pl.when(t == 0)
    def _():
        s_ref[...] = jnp.zeros_like(s_ref)
        tail_ref[...] = jnp.zeros_like(tail_ref)

    row = lax.broadcasted_iota(jnp.int32, (_PAIR, _PAIR), 0)
    col = lax.broadcasted_iota(jnp.int32, (_PAIR, _PAIR), 1)
    eye = row == col
    same_chunk = (row // _CHUNK) == (col // _CHUNK)
    causal = same_chunk & (row >= col)
    first_lanes = col < _CHUNK
    eye_f = jnp.where(eye, 1.0, 0.0).astype(_F32)
    pos = lax.broadcasted_iota(jnp.int32, (_SUBLANE, tl), 1) % _CHUNK
    zeros_half = jnp.zeros((_CHUNK, dk), _BF16)

    def conv_silu(x_ref, c_ref, hh, idx):
        x = x_ref[hh].astype(_F32)
        xb_ref[idx, 0:_SUBLANE, :] = tail_ref[idx]
        xb_ref[idx, _SUBLANE:, :] = x
        tail_ref[idx] = x[tl - _SUBLANE:, :]
        w = c_ref[hh]
        y = w[_CONV_TAPS - 1:_CONV_TAPS, :] * x
        for j in range(_CONV_TAPS - 1):
            off = _SUBLANE - (_CONV_TAPS - 1) + j
            y = y + w[j:j + 1, :] * xb_ref[idx, off:off + tl, :]
        return y * _sigmoid(y)

    def l2n(x):
        return x * lax.rsqrt(jnp.sum(x * x, axis=-1, keepdims=True) + _RMS_EPS)

    def to_col(r):
        return jnp.sum(jnp.where(eye, r, 0.0), axis=-1, keepdims=True)

    blocks = []
    for hh in range(hb):
        h = hg * hb + hh
        q = l2n(conv_silu(q_ref, cq_ref, hh, 3 * hh)) * (dk ** -0.5)
        k = l2n(conv_silu(k_ref, ck_ref, hh, 3 * hh + 1))
        v = conv_silu(v_ref, cv_ref, hh, 3 * hh + 2)

        beta_r = _sigmoid(bl_ref[hh])
        a_exp = jnp.exp(jnp.full((1, tl), ab_ref[0, h], _F32))
        g_r = -a_exp * _softplus(al_ref[hh] + ab_ref[1, h])
        g8 = jnp.broadcast_to(g_r, (_SUBLANE, tl))
        gc = g8
        sf = g8
        s_ = 1
        while s_ < _CHUNK:
            gc = gc + jnp.where(pos >= s_, pltpu.roll(gc, s_, axis=1), 0.0)
            sf = sf + jnp.where(pos < _CHUNK - s_, pltpu.roll(sf, tl - s_, axis=1), 0.0)
            s_ *= 2
        gc_r = gc[0:1, :]
        tail_r = jnp.exp(sf[0:1, :] - g_r)

        for p in range(npair):
            lo, hi = p * _PAIR, (p + 1) * _PAIR
            qp, kp, vp = q[lo:hi], k[lo:hi], v[lo:hi]
            beta_rp, gc_rp = beta_r[:, lo:hi], gc_r[:, lo:hi]
            beta_c, gc_c = to_col(beta_rp), to_col(gc_rp)
            egc_c = jnp.exp(gc_c)
            kpb = kp.astype(_BF16)
            qk = _mm_nt(jnp.concatenate([qp.astype(_BF16), kpb], axis=0), kpb)
            dm = jnp.exp(jnp.where(causal, gc_c - gc_rp, _NEG))
            a = jnp.where(eye, 0.0, qk[_PAIR:] * beta_c * dm)
            qd = (qk[:_PAIR] * dm).astype(_BF16)
            x = eye_f - a
            apow = a.astype(_BF16)
            s_ = 2
            while s_ < _CHUNK:
                apow = _mm(apow, apow).astype(_BF16)
                x = x + _mm(x.astype(_BF16), apow)
                s_ *= 2
            rhs = jnp.concatenate([vp * beta_c, kp * (beta_c * egc_c)], axis=1).astype(_BF16)
            sol = _mm(x.astype(_BF16), rhs)
            u, w = sol[:, :dk], sol[:, dk:].astype(_BF16)
            qdec = (qp * egc_c).astype(_BF16)
            kt = kp.T * tail_r[:, lo:hi]
            blocks.append(dict(
                u=u, qd=qd,
                wq1=jnp.concatenate([w[:_CHUNK], qdec[:_CHUNK]], axis=0),
                wq2=jnp.concatenate([w[_CHUNK:], qdec[_CHUNK:]], axis=0),
                kt1=jnp.where(first_lanes, kt, 0.0).astype(_BF16),
                kt2=jnp.where(first_lanes, 0.0, kt).astype(_BF16),
                gt1=egc_c[_CHUNK - 1:_CHUNK, :], gt2=egc_c[_PAIR - 1:_PAIR, :]))

    nw = nw_ref[...]
    for hh in range(hb):
        s = s_ref[hh]
        for p in range(npair):
            b = blocks[hh * npair + p]
            r1 = _mm(b["wq1"], s.astype(_BF16))
            vn1 = (b["u"][:_CHUNK] - r1[:_CHUNK]).astype(_BF16)
            s = s * b["gt1"] + _mm(b["kt1"], jnp.concatenate([vn1, zeros_half], axis=0))
            r2 = _mm(b["wq2"], s.astype(_BF16))
            vn2 = (b["u"][_CHUNK:] - r2[:_CHUNK]).astype(_BF16)
            vn = jnp.concatenate([vn1, vn2], axis=0)
            s = s * b["gt2"] + _mm(b["kt2"], vn)
            o = jnp.concatenate([r1[_CHUNK:], r2[_CHUNK:]], axis=0) + _mm(b["qd"], vn)
            lo, hi = p * _PAIR, (p + 1) * _PAIR
            zf = z_ref[hh, lo:hi, :].astype(_F32)
            o = o * lax.rsqrt(jnp.mean(o * o, axis=-1, keepdims=True) + _RMS_EPS) * nw
            o_ref[lo:hi, hh * dk:(hh + 1) * dk] = (o * (zf * _sigmoid(zf))).astype(o_ref.dtype)
        s_ref[hh] = s


def _deltanet(hcm, rows, conv_t, ab, norm_w, *, batch, seq, heads, blk_q, blk_k, blk_v, blk_z):
    dk = hcm.shape[2]
    m = hcm.shape[1]
    hb = 2
    tl = _tile(seq, 512)
    nt = seq // tl

    def act(off):
        return pl.BlockSpec((hb, tl, dk), lambda b, g, t: (off // hb + g, b * nt + t, 0))

    def taps(off):
        return pl.BlockSpec((hb, _CONV_TAPS, dk), lambda b, g, t: (off // hb + g, 0, 0))

    def logit(off):
        return pl.BlockSpec((hb, 1, tl), lambda b, g, t: (off // hb + g, 0, b * nt + t))

    return pl.pallas_call(
        functools.partial(_dn_body, hb=hb, tl=tl, dk=dk),
        grid=(batch, heads // hb, nt),
        in_specs=[pl.BlockSpec(memory_space=pltpu.SMEM),
                  act(blk_q), act(blk_k), act(blk_v), act(blk_z),
                  taps(0), taps(heads), taps(2 * heads),
                  logit(0), logit(heads),
                  pl.BlockSpec((1, dk), lambda b, g, t: (0, 0))],
        out_specs=pl.BlockSpec((tl, hb * dk), lambda b, g, t: (b * nt + t, g)),
        out_shape=jax.ShapeDtypeStruct((m, heads * dk), _BF16),
        scratch_shapes=[pltpu.VMEM((hb, dk, dk), _F32),
                        pltpu.VMEM((3 * hb, _SUBLANE, dk), _F32),
                        pltpu.VMEM((3 * hb, tl + _SUBLANE, dk), _F32)],
        compiler_params=_cparams(("parallel", "parallel", "arbitrary")),
        name="gated_deltanet",
    )(ab, hcm, hcm, hcm, hcm, conv_t, conv_t, conv_t, rows, rows, norm_w)


def _attn_body(li_ref, q_ref, k_ref, v_ref, lam_ref, nw_ref, o_ref, qs_ref, m_ref, l_ref, acc_ref,
               *, heads, tq, d):
    h = pl.program_id(1)
    i = pl.program_id(2)
    tk = tq
    dv = 2 * d
    lam_init = li_ref[0]

    lane = lax.broadcasted_iota(jnp.int32, (tq, dv), 1)
    qf = q_ref[0].astype(_F32) * (d ** -0.5)
    qs_ref[0:tq, :] = jnp.where(lane < d, qf, 0.0).astype(_BF16)
    qs_ref[tq:, :] = jnp.where(lane < d, 0.0, qf).astype(_BF16)
    m_ref[...] = jnp.full_like(m_ref, _NEG)
    l_ref[...] = jnp.zeros_like(l_ref)
    acc_ref[...] = jnp.zeros_like(acc_ref)

    hf = (h + 1).astype(_F32)
    slope = jnp.exp2(jnp.full((1, tk), -8.0 / heads, _F32) * hf)
    jl = lax.broadcasted_iota(jnp.int32, (1, tk), 1).astype(_F32)

    def update(kb, masked):
        start = pl.multiple_of(kb * tk, tk)
        k_t = k_ref[0, pl.ds(start, tk), :]
        v_t = v_ref[0, pl.ds(start, tk), :]
        s = _mm_nt(qs_ref[...], k_t)
        s = s + slope * (jl + (kb * tk - i * tq).astype(_F32))
        if masked:
            rpos = lax.broadcasted_iota(jnp.int32, (2 * tq, tk), 0) % tq
            cpos = lax.broadcasted_iota(jnp.int32, (2 * tq, tk), 1)
            s = jnp.where(rpos >= cpos, s, _NEG)
        m_old = m_ref[...]
        m_new = jnp.maximum(m_old, jnp.max(s, axis=-1, keepdims=True))
        p = jnp.exp(s - jnp.concatenate([m_new] * (tk // _LANE), axis=1))
        alpha = jnp.exp(m_old - m_new)
        l_ref[...] = alpha * l_ref[...] + jnp.sum(p, axis=-1, keepdims=True)
        acc_ref[...] = alpha * acc_ref[...] + _mm(p.astype(_BF16), v_t)
        m_ref[...] = m_new

    def body(kb, carry):
        update(kb, False)
        return carry

    lax.fori_loop(0, i, body, 0)
    update(i, True)

    lp = lam_ref[...]
    lam = (jnp.exp(jnp.sum(lp[0:1] * lp[1:2], axis=-1, keepdims=True))
           - jnp.exp(jnp.sum(lp[2:3] * lp[3:4], axis=-1, keepdims=True)) + lam_init)
    o1 = acc_ref[0:tq, :] / l_ref[0:tq, :]
    o2 = acc_ref[tq:, :] / l_ref[tq:, :]
    o = o1 - lam * o2
    o = o * lax.rsqrt(jnp.mean(o * o, axis=-1, keepdims=True) + _RMS_EPS) * nw_ref[...]
    o_ref[...] = (o * (1.0 - lam_init)).astype(o_ref.dtype)


def _diff_attention(hcm, lam_all, norm_w, lam_init, layer, *, batch, seq, heads, blk_q, blk_k, blk_v):
    dv = hcm.shape[2]
    d = dv // 2
    m = hcm.shape[1]
    tq = _tile(seq, 512)
    nq = seq // tq
    return pl.pallas_call(
        functools.partial(_attn_body, heads=heads, tq=tq, d=d),
        grid=(batch, heads, nq),
        in_specs=[pl.BlockSpec(memory_space=pltpu.SMEM),
                  pl.BlockSpec((1, tq, dv), lambda b, h, i: (blk_q + h, b * nq + i, 0)),
                  pl.BlockSpec((1, seq, dv), lambda b, h, i: (blk_k + h, b, 0)),
                  pl.BlockSpec((1, seq, dv), lambda b, h, i: (blk_v + h, b, 0)),
                  pl.BlockSpec((None, 4, d), lambda b, h, i: (layer, 0, 0)),
                  pl.BlockSpec((1, dv), lambda b, h, i: (0, 0))],
        out_specs=pl.BlockSpec((tq, dv), lambda b, h, i: (b * nq + i, h)),
        out_shape=jax.ShapeDtypeStruct((m, heads * dv), _BF16),
        scratch_shapes=[pltpu.VMEM((2 * tq, dv), _BF16),
                        pltpu.VMEM((2 * tq, _LANE), _F32),
                        pltpu.VMEM((2 * tq, _LANE), _F32),
                        pltpu.VMEM((2 * tq, dv), _F32)],
        compiler_params=_cparams(("parallel", "parallel", "arbitrary")),
        name="diff_attention",
    )(lam_init, hcm, hcm, hcm, lam_all, norm_w)


def _merge_body(oa_ref, ob_ref, wa_ref, wb_ref, ga_ref, gb_ref, o_ref):
    ya = _mm(oa_ref[...], wa_ref[...])
    yb = _mm(ob_ref[...], wb_ref[...])
    for j in range(ga_ref.shape[0]):
        sl = slice(j * _LANE, (j + 1) * _LANE)
        o_ref[:, sl] = (_sigmoid(ga_ref[j].astype(_F32)) * ya[:, sl]
                        + _sigmoid(gb_ref[j].astype(_F32)) * yb[:, sl]).astype(o_ref.dtype)


def _merge(oa, ob, wa_all, wb_all, hcm, layer, *, blk_ga, blk_gb):
    m, k = oa.shape
    n = wa_all.shape[2]
    tm, tn = _tile(m, 1024), _tile(n, 512)
    nb = tn // _LANE
    return pl.pallas_call(
        _merge_body,
        grid=(m // tm, n // tn),
        in_specs=[pl.BlockSpec((tm, k), lambda i, j: (i, 0)),
                  pl.BlockSpec((tm, k), lambda i, j: (i, 0)),
                  pl.BlockSpec((None, k, tn), lambda i, j: (layer, 0, j)),
                  pl.BlockSpec((None, k, tn), lambda i, j: (layer, 0, j)),
                  pl.BlockSpec((nb, tm, _LANE), lambda i, j: (blk_ga // nb + j, i, 0)),
                  pl.BlockSpec((nb, tm, _LANE), lambda i, j: (blk_gb // nb + j, i, 0))],
        out_specs=pl.BlockSpec((tm, tn), lambda i, j: (i, j)),
        out_shape=jax.ShapeDtypeStruct((m, n), _BF16),
        compiler_params=_cparams(("parallel", "arbitrary")),
        name="branch_merge",
    )(oa, ob, wa_all, wb_all, hcm, hcm)


def _mm_res_ln_body(a_ref, w_ref, x_ref, g_ref, b_ref, of_ref, ob_ref, acc_ref, *, alpha, nk):
    kk = pl.program_id(1)

    @pl.when(kk == 0)
    def _():
        acc_ref[...] = jnp.zeros_like(acc_ref)

    acc_ref[...] += _mm(a_ref[...], w_ref[...])

    @pl.when(kk == nk - 1)
    def _():
        y = alpha * x_ref[...] + acc_ref[...]
        mu = jnp.mean(y, axis=-1, keepdims=True)
        yc = y - mu
        var = jnp.mean(yc * yc, axis=-1, keepdims=True)
        out = yc * lax.rsqrt(var + _LN_EPS) * g_ref[...] + b_ref[...]
        of_ref[...] = out
        ob_ref[...] = out.astype(ob_ref.dtype)


def _mm_res_ln(a, w_all, x, ln_g, ln_b, layer, which, alpha, tk_pref):
    m, k = a.shape
    n = w_all.shape[2]
    tm, tk = _tile(m, 512), _tile(k, tk_pref)
    nk = k // tk
    return pl.pallas_call(
        functools.partial(_mm_res_ln_body, alpha=alpha, nk=nk),
        grid=(m // tm, nk),
        in_specs=[pl.BlockSpec((tm, tk), lambda i, j: (i, j)),
                  pl.BlockSpec((None, tk, n), lambda i, j: (layer, j, 0)),
                  pl.BlockSpec((tm, n), lambda i, j: (i, 0)),
                  pl.BlockSpec((None, 1, n), lambda i, j: (2 * layer + which, 0, 0)),
                  pl.BlockSpec((None, 1, n), lambda i, j: (2 * layer + which, 0, 0))],
        out_specs=[pl.BlockSpec((tm, n), lambda i, j: (i, 0)),
                   pl.BlockSpec((tm, n), lambda i, j: (i, 0))],
        out_shape=[jax.ShapeDtypeStruct((m, n), _F32), jax.ShapeDtypeStruct((m, n), _BF16)],
        scratch_shapes=[pltpu.VMEM((tm, n), _F32)],
        compiler_params=_cparams(("parallel", "arbitrary")),
        name="matmul_residual_layernorm",
    )(a, w_all, x, ln_g, ln_b)


def _swiglu_body(x_ref, wg_ref, wu_ref, o_ref):
    x = x_ref[...]
    g = _mm(x, wg_ref[...])
    u = _mm(x, wu_ref[...])
    o_ref[...] = (g * _sigmoid(g) * u).astype(o_ref.dtype)


def _swiglu(xb, wg_all, wu_all, layer):
    m, k = xb.shape
    n = wg_all.shape[2]
    tm, tn = _tile(m, 1024), _tile(n, 512)
    return pl.pallas_call(
        _swiglu_body,
        grid=(m // tm, n // tn),
        in_specs=[pl.BlockSpec((tm, k), lambda i, j: (i, 0)),
                  pl.BlockSpec((None, k, tn), lambda i, j: (layer, 0, j)),
                  pl.BlockSpec((None, k, tn), lambda i, j: (layer, 0, j))],
        out_specs=pl.BlockSpec((tm, tn), lambda i, j: (i, j)),
        out_shape=jax.ShapeDtypeStruct((m, n), _BF16),
        compiler_params=_cparams(("parallel", "arbitrary")),
        name="swiglu_gate_up",
    )(xb, wg_all, wu_all)


def kernel(x, w_in, conv_w, a_log, dt_bias, dn_norm_w, da_lambda, da_norm_w, w_branch_a, w_branch_b,
           w_out, ln_g, ln_b, w_gate, w_up, w_down):
    batch, seq, d_model = x.shape
    depth = w_in.shape[0]
    dn_heads = a_log.shape[1]
    dk = dn_norm_w.shape[1]
    d = da_lambda.shape[2]
    dv = da_norm_w.shape[1]
    da_heads = w_branch_b.shape[1] // dv
    dn_qk = dn_heads * dk
    da_qk = da_heads * 2 * d
    assert dk == _LANE and dv == _LANE and 2 * d == dv and dn_heads % 2 == 0
    assert w_in.shape[2] == 4 * dn_qk + 2 * dn_heads + 2 * da_qk + da_heads * dv + 2 * d_model
    m = batch * seq
    alpha = (2 * depth) ** 0.25

    small = slice(4 * dn_qk, 4 * dn_qk + 2 * dn_heads)
    w_main = jnp.concatenate([w_in[:, :, :small.start], w_in[:, :, small.stop:]], axis=2).astype(_BF16)
    w_rows = jnp.swapaxes(w_in[:, :, small], 1, 2).astype(_BF16)
    wa, wb, wo = w_branch_a.astype(_BF16), w_branch_b.astype(_BF16), w_out.astype(_BF16)
    wg, wu, wd = w_gate.astype(_BF16), w_up.astype(_BF16), w_down.astype(_BF16)
    ln_g2 = ln_g.reshape(2 * depth, 1, d_model)
    ln_b2 = ln_b.reshape(2 * depth, 1, d_model)

    nb_dn, nb_da, nb_model = dn_qk // _LANE, da_qk // _LANE, d_model // _LANE
    blk_dq, blk_dk_, blk_dv, blk_dz = 0, nb_dn, 2 * nb_dn, 3 * nb_dn
    blk_aq = 4 * nb_dn
    blk_ak, blk_av = blk_aq + nb_da, blk_aq + 2 * nb_da
    blk_ga = blk_av + da_heads
    blk_gb = blk_ga + nb_model

    xf = x.reshape(m, d_model)
    xb = xf.astype(_BF16)
    for layer in range(depth):
        hcm = _project(xb, w_main, layer)
        rows = _project_rows(xb, w_rows, layer).reshape(2 * dn_heads, 1, m)
        conv_t = jnp.swapaxes(conv_w[layer].reshape(_CONV_TAPS, 3 * dn_heads, dk), 0, 1)
        ab = jnp.stack([a_log[layer], dt_bias[layer]])
        o_a = _deltanet(hcm, rows, conv_t, ab, dn_norm_w[layer].reshape(1, dk),
                        batch=batch, seq=seq, heads=dn_heads,
                        blk_q=blk_dq, blk_k=blk_dk_, blk_v=blk_dv, blk_z=blk_dz)
        lam_init = jnp.full((1,), 0.8 - 0.6 * math.exp(-0.3 * layer), _F32)
        o_b = _diff_attention(hcm, da_lambda, da_norm_w[layer].reshape(1, dv), lam_init, layer,
                              batch=batch, seq=seq, heads=da_heads,
                              blk_q=blk_aq, blk_k=blk_ak, blk_v=blk_av)
        merged = _merge(o_a, o_b, wa, wb, hcm, layer, blk_ga=blk_ga, blk_gb=blk_gb)
        xf, xb = _mm_res_ln(merged, wo, xf, ln_g2, ln_b2, layer, 0, alpha, 1024)
        act = _swiglu(xb, wg, wu, layer)
        xf, xb = _mm_res_ln(act, wd, xf, ln_g2, ln_b2, layer, 1, alpha, 512)
    return xf.reshape(batch, seq, d_model)
```

```python
import functools
import math

import jax
import jax.numpy as jnp
from jax import lax
from jax.experimental import pallas as pl
from jax.experimental.pallas import tpu as pltpu

_F32 = jnp.float32
_BF16 = jnp.bfloat16
_LANE = 128
_SUBLANE = 8
_VMEM_LIMIT = 56 * 1024 * 1024
_CHUNK = 128
_INV_BASE = 16
_CONV_TAPS = 4
_LN_EPS = 1e-5
_RMS_EPS = 1e-6
_NEG = -1e30
_LOG2E = 1.4426950408889634
_BIAS_COLS = 3

_NT = (((1,), (1,)), ((), ()))


def _mm(a, b):
    return jnp.dot(a, b, preferred_element_type=_F32)


def _mm_nt(a, b):
    return lax.dot_general(a, b, _NT, preferred_element_type=_F32)


def _cparams(semantics):
    return pltpu.CompilerParams(dimension_semantics=semantics, vmem_limit_bytes=_VMEM_LIMIT)


def _sigmoid(x):
    return 1.0 / (1.0 + jnp.exp(-x))


def _softplus(x):
    return jnp.maximum(x, 0.0) + jnp.log(1.0 + jnp.exp(-jnp.abs(x)))


def _tile(n, pref):
    t = min(n, pref)
    while n % t:
        t -= _LANE
    return t


def _round_robin(chains):
    chains = list(chains)
    while chains:
        alive = []
        for g in chains:
            try:
                next(g)
                alive.append(g)
            except StopIteration:
                pass
        chains = alive


def _proj_body(x_ref, w_ref, o_ref):
    acc = _mm(x_ref[...], w_ref[...])
    for j in range(o_ref.shape[0]):
        o_ref[j] = acc[:, j * _LANE:(j + 1) * _LANE].astype(o_ref.dtype)


def _project(xb, w_all, layer):
    m, k = xb.shape
    n = w_all.shape[2]
    tm, tn = _tile(m, 1024), _tile(n, 512)
    return pl.pallas_call(
        _proj_body,
        grid=(m // tm, n // tn),
        in_specs=[pl.BlockSpec((tm, k), lambda i, j: (i, 0)),
                  pl.BlockSpec((None, k, tn), lambda i, j: (layer, 0, j))],
        out_specs=pl.BlockSpec((tn // _LANE, tm, _LANE), lambda i, j: (j, i, 0)),
        out_shape=jax.ShapeDtypeStruct((n // _LANE, m, _LANE), _BF16),
        compiler_params=_cparams(("parallel", "arbitrary")),
        name="in_proj",
    )(xb, w_all)


def _rowproj_body(w_ref, x_ref, o_ref):
    o_ref[...] = _mm_nt(w_ref[...], x_ref[...])


def _project_rows(xb, wt_all, layer):
    m, k = xb.shape
    r = wt_all.shape[1]
    tm = _tile(m, 1024)
    return pl.pallas_call(
        _rowproj_body,
        grid=(m // tm,),
        in_specs=[pl.BlockSpec((None, r, k), lambda i: (layer, 0, 0)),
                  pl.BlockSpec((tm, k), lambda i: (i, 0))],
        out_specs=pl.BlockSpec((r, tm), lambda i: (0, i)),
        out_shape=jax.ShapeDtypeStruct((r, m), _F32),
        compiler_params=_cparams(("parallel",)),
        name="gate_logits",
    )(wt_all, xb)


def _dn_body(ab_ref, q_ref, k_ref, v_ref, z_ref, cq_ref, ck_ref, cv_ref, bl_ref, al_ref, nw_ref,
             o_ref, s_ref, tail_ref, xb_ref, *, hb, tl, dk):
    hg = pl.program_id(1)
    t = pl.program_id(2)
    c = _CHUNK
    nchunk = tl // c

    @pl.when(t == 0)
    def _():
        s_ref[...] = jnp.zeros_like(s_ref)
        tail_ref[...] = jnp.zeros_like(tail_ref)

    row = lax.broadcasted_iota(jnp.int32, (c, c), 0)
    col = lax.broadcasted_iota(jnp.int32, (c, c), 1)
    eye = row == col
    lower = row >= col
    eye_f = jnp.where(eye, 1.0, 0.0).astype(_F32)
    widths = []
    w_ = _INV_BASE
    while w_ < c:
        widths.append(w_)
        w_ *= 2
    same_blk = [(row // w_) == (col // w_) for w_ in widths]
    pos = lax.broadcasted_iota(jnp.int32, (_SUBLANE, tl), 1) % c

    def conv_silu(x_ref, c_ref, hh, idx):
        x = x_ref[hh].astype(_F32)
        xb_ref[idx, 0:_SUBLANE, :] = tail_ref[idx]
        xb_ref[idx, _SUBLANE:, :] = x
        tail_ref[idx] = x[tl - _SUBLANE:, :]
        w = c_ref[hh]
        y = w[_CONV_TAPS - 1:_CONV_TAPS, :] * x
        for j in range(_CONV_TAPS - 1):
            off = _SUBLANE - (_CONV_TAPS - 1) + j
            y = y + w[j:j + 1, :] * xb_ref[idx, off:off + tl, :]
        return y * _sigmoid(y)

    def l2n(x):
        return x * lax.rsqrt(jnp.sum(x * x, axis=-1, keepdims=True) + _RMS_EPS)

    def to_col(r):
        return jnp.sum(jnp.where(eye, r, 0.0), axis=-1, keepdims=True)

    heads = []
    for hh in range(hb):
        h = hg * hb + hh
        q = l2n(conv_silu(q_ref, cq_ref, hh, 3 * hh)) * (dk ** -0.5)
        k = l2n(conv_silu(k_ref, ck_ref, hh, 3 * hh + 1))
        v = conv_silu(v_ref, cv_ref, hh, 3 * hh + 2)
        beta_r = _sigmoid(bl_ref[hh])
        a_exp = jnp.exp(jnp.full((1, tl), ab_ref[0, h], _F32))
        g_r = -a_exp * _softplus(al_ref[hh] + ab_ref[1, h])
        g8 = jnp.broadcast_to(g_r, (_SUBLANE, tl))
        gc = g8
        sf = g8
        s_ = 1
        while s_ < c:
            gc = gc + jnp.where(pos >= s_, pltpu.roll(gc, s_, axis=1), 0.0)
            sf = sf + jnp.where(pos < c - s_, pltpu.roll(sf, tl - s_, axis=1), 0.0)
            s_ *= 2
        heads.append(dict(q=q, k=k, v=v, beta_r=beta_r, gc_r=gc[0:1, :],
                          tail_r=jnp.exp(sf[0:1, :] - g_r)))

    blocks = {}

    def chunk_chain(hh, ci):
        hd = heads[hh]
        lo, hi = ci * c, (ci + 1) * c
        qp, kp, vp = hd["q"][lo:hi], hd["k"][lo:hi], hd["v"][lo:hi]
        gc_rp = hd["gc_r"][:, lo:hi]
        beta_c, gc_c = to_col(hd["beta_r"][:, lo:hi]), to_col(gc_rp)
        egc_c = jnp.exp(gc_c)
        kpb = kp.astype(_BF16)
        qk = _mm_nt(jnp.concatenate([qp.astype(_BF16), kpb], axis=0), kpb)
        yield
        dm = jnp.exp(jnp.where(lower, gc_c - gc_rp, _NEG))
        a = jnp.where(eye, 0.0, qk[c:] * beta_c * dm)
        qd = (qk[:c] * dm).astype(_BF16)
        ad = jnp.where(same_blk[0], a, 0.0)
        x = eye_f - ad
        ap = ad.astype(_BF16)
        w_ = 2
        while w_ < _INV_BASE:
            ap = _mm(ap, ap).astype(_BF16)
            yield
            x = x + _mm(x.astype(_BF16), ap)
            yield
            w_ *= 2
        for lvl in range(len(widths)):
            inner = same_blk[lvl]
            off = (a if lvl + 1 == len(widths) else jnp.where(same_blk[lvl + 1], a, 0.0))
            off = jnp.where(inner, 0.0, off).astype(_BF16)
            xb = x.astype(_BF16)
            t1 = _mm(xb, off).astype(_BF16)
            yield
            x = x - _mm(t1, xb)
            yield
        rhs = jnp.concatenate([vp * beta_c, kp * (beta_c * egc_c)], axis=1).astype(_BF16)
        sol = _mm(x.astype(_BF16), rhs)
        yield
        blocks[(hh, ci)] = dict(
            u=sol[:, :dk], qd=qd,
            wq=jnp.concatenate([sol[:, dk:].astype(_BF16), (qp * egc_c).astype(_BF16)], axis=0),
            kt=(kp.T * hd["tail_r"][:, lo:hi]).astype(_BF16),
            gt=egc_c[c - 1:c, :])

    _round_robin([chunk_chain(hh, ci) for ci in range(nchunk) for hh in range(hb)])

    nw = nw_ref[...]

    def state_chain(hh):
        s = s_ref[hh]
        for ci in range(nchunk):
            b = blocks[(hh, ci)]
            r = _mm(b["wq"], s.astype(_BF16))
            yield
            vn = (b["u"] - r[:c]).astype(_BF16)
            s = s * b["gt"] + _mm(b["kt"], vn)
            yield
            o = r[c:] + _mm(b["qd"], vn)
            yield
            lo, hi = ci * c, (ci + 1) * c
            zf = z_ref[hh, lo:hi, :].astype(_F32)
            o = o * lax.rsqrt(jnp.mean(o * o, axis=-1, keepdims=True) + _RMS_EPS) * nw
            o_ref[lo:hi, hh * dk:(hh + 1) * dk] = (o * (zf * _sigmoid(zf))).astype(o_ref.dtype)
        s_ref[hh] = s

    _round_robin([state_chain(hh) for hh in range(hb)])


def _deltanet(hcm, rows, conv_t, ab, norm_w, *, batch, seq, heads, blk_q, blk_k, blk_v, blk_z):
    dk = hcm.shape[2]
    m = hcm.shape[1]
    hb = 4 if heads % 4 == 0 else 2
    tl = _tile(seq, 512)
    nt = seq // tl

    def act(off):
        return pl.BlockSpec((hb, tl, dk), lambda b, g, t: (off // hb + g, b * nt + t, 0))

    def taps(off):
        return pl.BlockSpec((hb, _CONV_TAPS, dk), lambda b, g, t: (off // hb + g, 0, 0))

    def logit(off):
        return pl.BlockSpec((hb, 1, tl), lambda b, g, t: (off // hb + g, 0, b * nt + t))

    return pl.pallas_call(
        functools.partial(_dn_body, hb=hb, tl=tl, dk=dk),
        grid=(batch, heads // hb, nt),
        in_specs=[pl.BlockSpec(memory_space=pltpu.SMEM),
                  act(blk_q), act(blk_k), act(blk_v), act(blk_z),
                  taps(0), taps(heads), taps(2 * heads),
                  logit(0), logit(heads),
                  pl.BlockSpec((1, dk), lambda b, g, t: (0, 0))],
        out_specs=pl.BlockSpec((tl, hb * dk), lambda b, g, t: (b * nt + t, g)),
        out_shape=jax.ShapeDtypeStruct((m, heads * dk), _BF16),
        scratch_shapes=[pltpu.VMEM((hb, dk, dk), _F32),
                        pltpu.VMEM((3 * hb, _SUBLANE, dk), _F32),
                        pltpu.VMEM((3 * hb, tl + _SUBLANE, dk), _F32)],
        compiler_params=_cparams(("parallel", "parallel", "arbitrary")),
        name="gated_deltanet",
    )(ab, hcm, hcm, hcm, hcm, conv_t, conv_t, conv_t, rows, rows, norm_w)


def _attn_body(li_ref, q_ref, k_ref, v_ref, lam_ref, nw_ref, o_ref, kx_ref, vx_ref, qs_ref, m_ref, acc_ref,
               s0_ref, s1_ref, *, heads, tq, d, seq):
    h = pl.program_id(1)
    i = pl.program_id(2)
    tk = tq // 2
    dv = 2 * d
    kw = 2 * dv
    lam_init = li_ref[0]

    @pl.when(i == 0)
    def _():
        hf = (h + 1).astype(_F32)
        slope2 = jnp.exp2(jnp.full((1, dv), -8.0 / heads, _F32) * hf) * _LOG2E
        jpos = lax.broadcasted_iota(jnp.int32, (seq, dv), 0).astype(_F32) - float(seq)
        lane = lax.broadcasted_iota(jnp.int32, (seq, dv), 1)
        rest = jpos * slope2
        ext = jnp.zeros((seq, dv), _F32)
        for piece in range(_BIAS_COLS):
            part = rest.astype(_BF16).astype(_F32)
            rest = rest - part
            ext = jnp.where(lane == piece, part, ext)
        kx_ref[:, :dv] = k_ref[0]
        kx_ref[:, dv:] = ext.astype(_BF16)
        vx_ref[:, :dv] = v_ref[0]
        vx_ref[:, dv:] = jnp.ones((seq, dv), _BF16)

    lane = lax.broadcasted_iota(jnp.int32, (tq, kw), 1)
    qf = q_ref[0].astype(_F32) * (d ** -0.5 * _LOG2E)
    qf = jnp.concatenate([qf, jnp.zeros((tq, dv), _F32)], axis=1)
    ones = (lane >= dv) & (lane < dv + _BIAS_COLS)
    qs_ref[0:tq, :] = jnp.where(ones, 1.0, jnp.where(lane < d, qf, 0.0)).astype(_BF16)
    qs_ref[tq:, :] = jnp.where(ones, 1.0, jnp.where((lane >= d) & (lane < dv), qf, 0.0)).astype(_BF16)
    m_ref[...] = jnp.full_like(m_ref, _NEG)
    acc_ref[...] = jnp.zeros_like(acc_ref)

    def logits(t, s_ref):
        start = pl.multiple_of(t * tk, tk)
        s_ref[...] = _mm_nt(qs_ref[...], kx_ref[pl.ds(start, tk), :])

    def absorb(t, s_ref, diag_off):
        start = pl.multiple_of(t * tk, tk)
        s = s_ref[...]
        if diag_off is not None:
            rpos = lax.broadcasted_iota(jnp.int32, (2 * tq, tk), 0) % tq
            cpos = lax.broadcasted_iota(jnp.int32, (2 * tq, tk), 1) + diag_off
            s = jnp.where(rpos >= cpos, s, _NEG)
        m_old = m_ref[...]
        m_new = jnp.maximum(m_old, jnp.max(s, axis=-1, keepdims=True))
        p = jnp.exp2(s - jnp.concatenate([m_new] * (tk // _LANE), axis=1))
        alpha = jnp.exp2(m_old - m_new)
        acc_ref[...] = (jnp.concatenate([alpha, alpha], axis=1) * acc_ref[...]
                        + _mm(p.astype(_BF16), vx_ref[pl.ds(start, tk), :]))
        m_ref[...] = m_new

    def body(kb, carry):
        logits(2 * kb + 1, s1_ref)
        absorb(2 * kb, s0_ref, None)
        logits(2 * kb + 2, s0_ref)
        absorb(2 * kb + 1, s1_ref, None)
        return carry

    logits(0, s0_ref)
    lax.fori_loop(0, i, body, 0)
    logits(2 * i + 1, s1_ref)
    absorb(2 * i, s0_ref, 0)
    absorb(2 * i + 1, s1_ref, tk)

    lp = lam_ref[...]
    lam = (jnp.exp(jnp.sum(lp[0:1] * lp[1:2], axis=-1, keepdims=True))
           - jnp.exp(jnp.sum(lp[2:3] * lp[3:4], axis=-1, keepdims=True)) + lam_init)
    o1 = acc_ref[0:tq, 0:dv] / acc_ref[0:tq, dv:]
    o2 = acc_ref[tq:, 0:dv] / acc_ref[tq:, dv:]
    o = o1 - lam * o2
    o = o * lax.rsqrt(jnp.mean(o * o, axis=-1, keepdims=True) + _RMS_EPS) * nw_ref[...]
    o_ref[...] = (o * (1.0 - lam_init)).astype(o_ref.dtype)


def _diff_attention(hcm, lam_all, norm_w, lam_init, layer, *, batch, seq, heads, blk_q, blk_k, blk_v):
    dv = hcm.shape[2]
    d = dv // 2
    m = hcm.shape[1]
    tq = _tile(seq, 512)
    nq = seq // tq
    return pl.pallas_call(
        functools.partial(_attn_body, heads=heads, tq=tq, d=d, seq=seq),
        grid=(batch, heads, nq),
        in_specs=[pl.BlockSpec(memory_space=pltpu.SMEM),
                  pl.BlockSpec((1, tq, dv), lambda b, h, i: (blk_q + h, b * nq + i, 0)),
                  pl.BlockSpec((1, seq, dv), lambda b, h, i: (blk_k + h, b, 0)),
                  pl.BlockSpec((1, seq, dv), lambda b, h, i: (blk_v + h, b, 0)),
                  pl.BlockSpec((None, 4, d), lambda b, h, i: (layer, 0, 0)),
                  pl.BlockSpec((1, dv), lambda b, h, i: (0, 0))],
        out_specs=pl.BlockSpec((tq, dv), lambda b, h, i: (b * nq + i, h)),
        out_shape=jax.ShapeDtypeStruct((m, heads * dv), _BF16),
        scratch_shapes=[pltpu.VMEM((seq, 2 * dv), _BF16),
                        pltpu.VMEM((seq, 2 * dv), _BF16),
                        pltpu.VMEM((2 * tq, 2 * dv), _BF16),
                        pltpu.VMEM((2 * tq, _LANE), _F32),
                        pltpu.VMEM((2 * tq, 2 * dv), _F32),
                        pltpu.VMEM((2 * tq, tq // 2), _F32),
                        pltpu.VMEM((2 * tq, tq // 2), _F32)],
        compiler_params=_cparams(("parallel", "parallel", "arbitrary")),
        name="diff_attention",
    )(lam_init, hcm, hcm, hcm, lam_all, norm_w)


def _merge_body(oa_ref, ob_ref, wa_ref, wb_ref, ga_ref, gb_ref, o_ref):
    ya = _mm(oa_ref[...], wa_ref[...])
    yb = _mm(ob_ref[...], wb_ref[...])
    for j in range(ga_ref.shape[0]):
        sl = slice(j * _LANE, (j + 1) * _LANE)
        o_ref[:, sl] = (_sigmoid(ga_ref[j].astype(_F32)) * ya[:, sl]
                        + _sigmoid(gb_ref[j].astype(_F32)) * yb[:, sl]).astype(o_ref.dtype)


def _merge(oa, ob, wa_all, wb_all, hcm, layer, *, blk_ga, blk_gb):
    m, k = oa.shape
    n = wa_all.shape[2]
    tm, tn = _tile(m, 1024), _tile(n, 512)
    nb = tn // _LANE
    assert blk_ga % nb == 0 and blk_gb % nb == 0
    return pl.pallas_call(
        _merge_body,
        grid=(m // tm, n // tn),
        in_specs=[pl.BlockSpec((tm, k), lambda i, j: (i, 0)),
                  pl.BlockSpec((tm, k), lambda i, j: (i, 0)),
                  pl.BlockSpec((None, k, tn), lambda i, j: (layer, 0, j)),
                  pl.BlockSpec((None, k, tn), lambda i, j: (layer, 0, j)),
                  pl.BlockSpec((nb, tm, _LANE), lambda i, j: (blk_ga // nb + j, i, 0)),
                  pl.BlockSpec((nb, tm, _LANE), lambda i, j: (blk_gb // nb + j, i, 0))],
        out_specs=pl.BlockSpec((tm, tn), lambda i, j: (i, j)),
        out_shape=jax.ShapeDtypeStruct((m, n), _BF16),
        compiler_params=_cparams(("parallel", "arbitrary")),
        name="branch_merge",
    )(oa, ob, wa_all, wb_all, hcm, hcm)


def _mm_res_ln_body(a_ref, w_ref, x_ref, g_ref, b_ref, of_ref, ob_ref, acc_ref, *, alpha, nk):
    kk = pl.program_id(1)

    def epilogue(acc):
        y = alpha * x_ref[...] + acc
        mu = jnp.mean(y, axis=-1, keepdims=True)
        yc = y - mu
        var = jnp.mean(yc * yc, axis=-1, keepdims=True)
        out = yc * lax.rsqrt(var + _LN_EPS) * g_ref[...] + b_ref[...]
        of_ref[...] = out
        ob_ref[...] = out.astype(ob_ref.dtype)

    if nk == 1:
        epilogue(_mm(a_ref[...], w_ref[...]))
        return

    @pl.when(kk == 0)
    def _():
        acc_ref[...] = _mm(a_ref[...], w_ref[...])

    @pl.when((kk > 0) & (kk < nk - 1))
    def _():
        acc_ref[...] += _mm(a_ref[...], w_ref[...])

    @pl.when(kk == nk - 1)
    def _():
        epilogue(acc_ref[...] + _mm(a_ref[...], w_ref[...]))


def _mm_res_ln(a, w_all, x, ln_g, ln_b, layer, which, alpha, tk_pref):
    m, k = a.shape
    n = w_all.shape[2]
    tm, tk = _tile(m, 512), _tile(k, tk_pref)
    nk = k // tk
    return pl.pallas_call(
        functools.partial(_mm_res_ln_body, alpha=alpha, nk=nk),
        grid=(m // tm, nk),
        in_specs=[pl.BlockSpec((tm, tk), lambda i, j: (i, j)),
                  pl.BlockSpec((None, tk, n), lambda i, j: (layer, j, 0)),
                  pl.BlockSpec((tm, n), lambda i, j: (i, 0)),
                  pl.BlockSpec((None, 1, n), lambda i, j: (2 * layer + which, 0, 0)),
                  pl.BlockSpec((None, 1, n), lambda i, j: (2 * layer + which, 0, 0))],
        out_specs=[pl.BlockSpec((tm, n), lambda i, j: (i, 0)),
                   pl.BlockSpec((tm, n), lambda i, j: (i, 0))],
        out_shape=[jax.ShapeDtypeStruct((m, n), _F32), jax.ShapeDtypeStruct((m, n), _BF16)],
        scratch_shapes=[pltpu.VMEM((tm, n) if nk > 1 else (_SUBLANE, _LANE), _F32)],
        compiler_params=_cparams(("parallel", "arbitrary")),
        name="matmul_residual_layernorm",
    )(a, w_all, x, ln_g, ln_b)


def _swiglu_body(x_ref, wg_ref, wu_ref, o_ref):
    x = x_ref[...]
    g = _mm(x, wg_ref[...])
    u = _mm(x, wu_ref[...])
    o_ref[...] = (g * _sigmoid(g) * u).astype(o_ref.dtype)


def _swiglu(xb, wg_all, wu_all, layer):
    m, k = xb.shape
    n = wg_all.shape[2]
    tm, tn = _tile(m, 1024), _tile(n, 512)
    return pl.pallas_call(
        _swiglu_body,
        grid=(m // tm, n // tn),
        in_specs=[pl.BlockSpec((tm, k), lambda i, j: (i, 0)),
                  pl.BlockSpec((None, k, tn), lambda i, j: (layer, 0, j)),
                  pl.BlockSpec((None, k, tn), lambda i, j: (layer, 0, j))],
        out_specs=pl.BlockSpec((tm, tn), lambda i, j: (i, j)),
        out_shape=jax.ShapeDtypeStruct((m, n), _BF16),
        compiler_params=_cparams(("parallel", "arbitrary")),
        name="swiglu_gate_up",
    )(xb, wg_all, wu_all)


def kernel(x, w_in, conv_w, a_log, dt_bias, dn_norm_w, da_lambda, da_norm_w, w_branch_a, w_branch_b,
           w_out, ln_g, ln_b, w_gate, w_up, w_down):
    batch, seq, d_model = x.shape
    depth = w_in.shape[0]
    dn_heads = a_log.shape[1]
    dk = dn_norm_w.shape[1]
    d = da_lambda.shape[2]
    dv = da_norm_w.shape[1]
    da_heads = w_branch_b.shape[1] // dv
    dn_qk = dn_heads * dk
    da_qk = da_heads * 2 * d
    assert dk == _LANE and dv == _LANE and 2 * d == dv and dn_heads % 2 == 0
    assert w_in.shape[2] == 4 * dn_qk + 2 * dn_heads + 2 * da_qk + da_heads * dv + 2 * d_model
    m = batch * seq
    alpha = (2 * depth) ** 0.25

    small = slice(4 * dn_qk, 4 * dn_qk + 2 * dn_heads)
    w_dn = w_in[:, :, :small.start].astype(_BF16)
    w_rest = w_in[:, :, small.stop:].astype(_BF16)
    w_rows = jnp.swapaxes(w_in[:, :, small], 1, 2).astype(_BF16)
    wa, wb, wo = w_branch_a.astype(_BF16), w_branch_b.astype(_BF16), w_out.astype(_BF16)
    wg, wu, wd = w_gate.astype(_BF16), w_up.astype(_BF16), w_down.astype(_BF16)
    ln_g2 = ln_g.reshape(2 * depth, 1, d_model)
    ln_b2 = ln_b.reshape(2 * depth, 1, d_model)

    nb_dn, nb_da, nb_model = dn_qk // _LANE, da_qk // _LANE, d_model // _LANE
    blk_aq, blk_ak, blk_av = 0, nb_da, 2 * nb_da
    blk_ga = blk_av + da_heads
    blk_gb = blk_ga + nb_model

    xf = x.reshape(m, d_model)
    xb = xf.astype(_BF16)
    for layer in range(depth):
        h_dn = _project(xb, w_dn, layer)
        h_rest = _project(xb, w_rest, layer)
        rows = _project_rows(xb, w_rows, layer).reshape(2 * dn_heads, 1, m)
        conv_t = jnp.swapaxes(conv_w[layer].reshape(_CONV_TAPS, 3 * dn_heads, dk), 0, 1)
        ab = jnp.stack([a_log[layer], dt_bias[layer]])
        o_a = _deltanet(h_dn, rows, conv_t, ab, dn_norm_w[layer].reshape(1, dk),
                        batch=batch, seq=seq, heads=dn_heads,
                        blk_q=0, blk_k=nb_dn, blk_v=2 * nb_dn, blk_z=3 * nb_dn)
        lam_init = jnp.full((1,), 0.8 - 0.6 * math.exp(-0.3 * layer), _F32)
        o_b = _diff_attention(h_rest, da_lambda, da_norm_w[layer].reshape(1, dv), lam_init, layer,
                              batch=batch, seq=seq, heads=da_heads,
                              blk_q=blk_aq, blk_k=blk_ak, blk_v=blk_av)
        merged = _merge(o_a, o_b, wa, wb, h_rest, layer, blk_ga=blk_ga, blk_gb=blk_gb)
        xf, xb = _mm_res_ln(merged, wo, xf, ln_g2, ln_b2, layer, 0, alpha, 2048)
        act = _swiglu(xb, wg, wu, layer)
        xf, xb = _mm_res_ln(act, wd, xf, ln_g2, ln_b2, layer, 1, alpha, 1408)
    return xf.reshape(batch, seq, d_model)
```

```python
import functools
import math

import jax
import jax.numpy as jnp
from jax import lax
from jax.experimental import pallas as pl
from jax.experimental.pallas import tpu as pltpu

_F32 = jnp.float32
_BF16 = jnp.bfloat16
_LANE = 128
_SUBLANE = 8
_VMEM_LIMIT = 56 * 1024 * 1024
_CHUNK = 128
_INV_BASE = 16
_CONV_TAPS = 4
_LN_EPS = 1e-5
_RMS_EPS = 1e-6
_NEG = -1e30
_LOG2E = 1.4426950408889634
_BIAS_COLS = 3
_ROW_CHUNK = 64

_NT = (((1,), (1,)), ((), ()))


def _mm(a, b):
    return jnp.dot(a, b, preferred_element_type=_F32)


def _mm_nt(a, b):
    return lax.dot_general(a, b, _NT, preferred_element_type=_F32)


def _cparams(semantics):
    return pltpu.CompilerParams(dimension_semantics=semantics, vmem_limit_bytes=_VMEM_LIMIT)


def _sigmoid(x):
    return 1.0 / (1.0 + jnp.exp(-x))


def _softplus(x):
    return jnp.maximum(x, 0.0) + jnp.log(1.0 + jnp.exp(-jnp.abs(x)))


def _tile(n, pref):
    t = min(n, pref)
    while n % t:
        t -= _LANE
    return t


def _round_robin(chains):
    chains = list(chains)
    while chains:
        alive = []
        for g in chains:
            try:
                next(g)
                alive.append(g)
            except StopIteration:
                pass
        chains = alive


def _proj_body(*refs, shift):
    if shift:
        x_ref, w_ref, wnext_ref, o_ref, wb_ref = refs
    else:
        x_ref, w_ref, o_ref, wb_ref = refs

    @pl.when(pl.program_id(1) == 0)
    def _():
        w = w_ref[...]
        if shift:
            tn = w.shape[1]
            lane = lax.broadcasted_iota(jnp.int32, w.shape, 1)
            w = jnp.where(lane < tn - shift, pltpu.roll(w, tn - shift, axis=1),
                          pltpu.roll(wnext_ref[...], tn - shift, axis=1))
        wb_ref[...] = w.astype(_BF16)

    acc = _mm(x_ref[...], wb_ref[...])
    for j in range(o_ref.shape[0]):
        o_ref[j] = acc[:, j * _LANE:(j + 1) * _LANE].astype(o_ref.dtype)


def _project(xb, w_all, layer, col0, n):
    m, k = xb.shape
    shift = col0 % _LANE
    tm, tn = _tile(m, 1024), _tile(n, 512)
    assert (col0 - shift) % tn == 0
    blk0 = (col0 - shift) // tn
    w_specs = [pl.BlockSpec((None, k, tn), lambda j, i: (layer, 0, blk0 + j))]
    if shift:
        w_specs.append(pl.BlockSpec((None, k, tn), lambda j, i: (layer, 0, blk0 + j + 1)))
    return pl.pallas_call(
        functools.partial(_proj_body, shift=shift),
        grid=(n // tn, m // tm),
        in_specs=[pl.BlockSpec((tm, k), lambda j, i: (i, 0))] + w_specs,
        out_specs=pl.BlockSpec((tn // _LANE, tm, _LANE), lambda j, i: (j, i, 0)),
        out_shape=jax.ShapeDtypeStruct((n // _LANE, m, _LANE), _BF16),
        scratch_shapes=[pltpu.VMEM((k, tn), _BF16)],
        compiler_params=_cparams(("parallel", "arbitrary")),
        name="in_proj",
    )(xb, *([w_all] * len(w_specs)))


def _rowproj_body(w_ref, x_ref, o_ref):
    o_ref[...] = _mm_nt(w_ref[...], x_ref[...])


def _project_rows(xb, wt_all, layer):
    m, k = xb.shape
    r = wt_all.shape[1]
    tm = _tile(m, 1024)
    return pl.pallas_call(
        _rowproj_body,
        grid=(m // tm,),
        in_specs=[pl.BlockSpec((None, r, k), lambda i: (layer, 0, 0)),
                  pl.BlockSpec((tm, k), lambda i: (i, 0))],
        out_specs=pl.BlockSpec((r, tm), lambda i: (0, i)),
        out_shape=jax.ShapeDtypeStruct((r, m), _F32),
        compiler_params=_cparams(("parallel",)),
        name="gate_logits",
    )(wt_all, xb)


def _dn_body(ab_ref, q_ref, k_ref, v_ref, z_ref, cq_ref, ck_ref, cv_ref, bl_ref, al_ref, nw_ref,
             o_ref, s_ref, tail_ref, xb_ref, *, hb, tl, dk):
    hg = pl.program_id(1)
    t = pl.program_id(2)
    c = _CHUNK
    nchunk = tl // c

    @pl.when(t == 0)
    def _():
        s_ref[...] = jnp.zeros_like(s_ref)
        tail_ref[...] = jnp.zeros_like(tail_ref)

    row = lax.broadcasted_iota(jnp.int32, (c, c), 0)
    col = lax.broadcasted_iota(jnp.int32, (c, c), 1)
    eye = row == col
    lower = row >= col
    eye_f = jnp.where(eye, 1.0, 0.0).astype(_F32)
    widths = []
    w_ = _INV_BASE
    while w_ < c:
        widths.append(w_)
        w_ *= 2
    same_blk = [(row // w_) == (col // w_) for w_ in widths]
    pos = lax.broadcasted_iota(jnp.int32, (_SUBLANE, tl), 1) % c

    def conv_silu(x_ref, c_ref, hh, idx):
        x = x_ref[hh].astype(_F32)
        xb_ref[idx, 0:_SUBLANE, :] = tail_ref[idx]
        xb_ref[idx, _SUBLANE:, :] = x
        tail_ref[idx] = x[tl - _SUBLANE:, :]
        w = c_ref[hh]
        y = w[_CONV_TAPS - 1:_CONV_TAPS, :] * x
        for j in range(_CONV_TAPS - 1):
            off = _SUBLANE - (_CONV_TAPS - 1) + j
            y = y + w[j:j + 1, :] * xb_ref[idx, off:off + tl, :]
        return y * _sigmoid(y)

    def l2n(x):
        return x * lax.rsqrt(jnp.sum(x * x, axis=-1, keepdims=True) + _RMS_EPS)

    def to_col(r):
        return jnp.sum(jnp.where(eye, r, 0.0), axis=-1, keepdims=True)

    heads = []
    for hh in range(hb):
        h = hg * hb + hh
        q = l2n(conv_silu(q_ref, cq_ref, hh, 3 * hh)) * (dk ** -0.5)
        k = l2n(conv_silu(k_ref, ck_ref, hh, 3 * hh + 1))
        v = conv_silu(v_ref, cv_ref, hh, 3 * hh + 2)
        beta_r = _sigmoid(bl_ref[hh])
        a_exp = jnp.exp(jnp.full((1, tl), ab_ref[0, h], _F32))
        g_r = -a_exp * _softplus(al_ref[hh] + ab_ref[1, h])
        g8 = jnp.broadcast_to(g_r, (_SUBLANE, tl))
        gc = g8
        sf = g8
        s_ = 1
        while s_ < c:
            gc = gc + jnp.where(pos >= s_, pltpu.roll(gc, s_, axis=1), 0.0)
            sf = sf + jnp.where(pos < c - s_, pltpu.roll(sf, tl - s_, axis=1), 0.0)
            s_ *= 2
        heads.append(dict(q=q, k=k, v=v, beta_r=beta_r, gc_r=gc[0:1, :],
                          tail_r=jnp.exp(sf[0:1, :] - g_r)))

    blocks = {}

    def chunk_chain(hh, ci):
        hd = heads[hh]
        lo, hi = ci * c, (ci + 1) * c
        qp, kp, vp = hd["q"][lo:hi], hd["k"][lo:hi], hd["v"][lo:hi]
        gc_rp = hd["gc_r"][:, lo:hi]
        beta_c, gc_c = to_col(hd["beta_r"][:, lo:hi]), to_col(gc_rp)
        egc_c = jnp.exp(gc_c)
        kpb = kp.astype(_BF16)
        qk = _mm_nt(jnp.concatenate([qp.astype(_BF16), kpb], axis=0), kpb)
        yield
        dm = jnp.exp(jnp.where(lower, gc_c - gc_rp, _NEG))
        a = jnp.where(eye, 0.0, qk[c:] * beta_c * dm)
        qd = (qk[:c] * dm).astype(_BF16)
        ad = jnp.where(same_blk[0], a, 0.0)
        x = eye_f - ad
        ap = ad.astype(_BF16)
        w_ = 2
        while w_ < _INV_BASE:
            ap = _mm(ap, ap).astype(_BF16)
            yield
            x = x + _mm(x.astype(_BF16), ap)
            yield
            w_ *= 2
        for lvl in range(len(widths)):
            inner = same_blk[lvl]
            off = (a if lvl + 1 == len(widths) else jnp.where(same_blk[lvl + 1], a, 0.0))
            off = jnp.where(inner, 0.0, off).astype(_BF16)
            xb = x.astype(_BF16)
            t1 = _mm(xb, off).astype(_BF16)
            yield
            x = x - _mm(t1, xb)
            yield
        rhs = jnp.concatenate([vp * beta_c, kp * (beta_c * egc_c)], axis=1).astype(_BF16)
        sol = _mm(x.astype(_BF16), rhs)
        yield
        blocks[(hh, ci)] = dict(
            u=sol[:, :dk], qd=qd,
            wq=jnp.concatenate([sol[:, dk:].astype(_BF16), (qp * egc_c).astype(_BF16)], axis=0),
            kt=(kp.T * hd["tail_r"][:, lo:hi]).astype(_BF16),
            gt=egc_c[c - 1:c, :])

    _round_robin([chunk_chain(hh, ci) for ci in range(nchunk) for hh in range(hb)])

    nw = nw_ref[...]

    def state_chain(hh):
        s = s_ref[hh]
        for ci in range(nchunk):
            b = blocks[(hh, ci)]
            r = _mm(b["wq"], s.astype(_BF16))
            yield
            vn = (b["u"] - r[:c]).astype(_BF16)
            s = s * b["gt"] + _mm(b["kt"], vn)
            yield
            o = r[c:] + _mm(b["qd"], vn)
            yield
            lo, hi = ci * c, (ci + 1) * c
            zf = z_ref[hh, lo:hi, :].astype(_F32)
            o = o * lax.rsqrt(jnp.mean(o * o, axis=-1, keepdims=True) + _RMS_EPS) * nw
            o_ref[lo:hi, hh * dk:(hh + 1) * dk] = (o * (zf * _sigmoid(zf))).astype(o_ref.dtype)
        s_ref[hh] = s

    _round_robin([state_chain(hh) for hh in range(hb)])


def _deltanet(hcm, rows, conv_t, ab, norm_w, *, batch, seq, heads, blk_q, blk_k, blk_v, blk_z):
    dk = hcm.shape[2]
    m = hcm.shape[1]
    hb = 4 if heads % 4 == 0 else 2
    tl = _tile(seq, 512)
    nt = seq // tl

    def act(off):
        return pl.BlockSpec((hb, tl, dk), lambda b, g, t: (off // hb + g, b * nt + t, 0))

    def taps(off):
        return pl.BlockSpec((hb, _CONV_TAPS, dk), lambda b, g, t: (off // hb + g, 0, 0))

    def logit(off):
        return pl.BlockSpec((hb, 1, tl), lambda b, g, t: (off // hb + g, 0, b * nt + t))

    return pl.pallas_call(
        functools.partial(_dn_body, hb=hb, tl=tl, dk=dk),
        grid=(batch, heads // hb, nt),
        in_specs=[pl.BlockSpec(memory_space=pltpu.SMEM),
                  act(blk_q), act(blk_k), act(blk_v), act(blk_z),
                  taps(0), taps(heads), taps(2 * heads),
                  logit(0), logit(heads),
                  pl.BlockSpec((1, dk), lambda b, g, t: (0, 0))],
        out_specs=pl.BlockSpec((tl, hb * dk), lambda b, g, t: (b * nt + t, g)),
        out_shape=jax.ShapeDtypeStruct((m, heads * dk), _BF16),
        scratch_shapes=[pltpu.VMEM((hb, dk, dk), _F32),
                        pltpu.VMEM((3 * hb, _SUBLANE, dk), _F32),
                        pltpu.VMEM((3 * hb, tl + _SUBLANE, dk), _F32)],
        compiler_params=_cparams(("parallel", "parallel", "arbitrary")),
        name="gated_deltanet",
    )(ab, hcm, hcm, hcm, hcm, conv_t, conv_t, conv_t, rows, rows, norm_w)


def _attn_body(li_ref, q_ref, k_ref, v_ref, lam_ref, nw_ref, o_ref, kx_ref, vx_ref, qs_ref, m_ref, acc_ref,
               s0_ref, s1_ref, cm_ref, p_ref, al_ref, *, heads, tq, d, seq):
    h = pl.program_id(1)
    hq = tq // 2
    tk = hq
    nq = seq // tq
    dv = 2 * d
    kw = 2 * dv
    lam_init = li_ref[0]

    def prepare_keys_values():
        hf = (h + 1).astype(_F32)
        slope2 = jnp.exp2(jnp.full((1, dv), -8.0 / heads, _F32) * hf) * _LOG2E
        rows = 4 * _ROW_CHUNK
        lane = lax.broadcasted_iota(jnp.int32, (rows, dv), 1)
        jloc = lax.broadcasted_iota(jnp.int32, (rows, dv), 0).astype(_F32)
        for r0 in range(0, seq, rows):
            rest = (jloc + float(r0 - seq)) * slope2
            ext = jnp.zeros((rows, dv), _F32)
            for piece in range(_BIAS_COLS):
                part = rest.astype(_BF16).astype(_F32)
                rest = rest - part
                ext = jnp.where(lane == piece, part, ext)
            kx_ref[r0:r0 + rows, dv:] = ext.astype(_BF16)
        kx_ref[:, :dv] = k_ref[0]
        vx_ref[:, :dv] = v_ref[0]
        vx_ref[:, dv:] = jnp.ones((seq, dv), _BF16)
        rpos = lax.broadcasted_iota(jnp.int32, (hq, tk), 0)
        cpos = lax.broadcasted_iota(jnp.int32, (hq, tk), 1)
        cm_ref[...] = jnp.where(rpos >= cpos, 0.0, _NEG).astype(_F32)

    def prepare_queries():
        lane = lax.broadcasted_iota(jnp.int32, (hq, kw), 1)
        ones = (lane >= dv) & (lane < dv + _BIAS_COLS)
        for g in range(2 * nq):
            qf = q_ref[0, g * hq:(g + 1) * hq, :].astype(_F32) * (d ** -0.5 * _LOG2E)
            qf = jnp.concatenate([qf, jnp.zeros((hq, dv), _F32)], axis=1)
            qs_ref[g * tq:g * tq + hq, :] = jnp.where(ones, 1.0, jnp.where(lane < d, qf, 0.0)).astype(_BF16)
            qs_ref[g * tq + hq:(g + 1) * tq, :] = jnp.where(
                ones, 1.0, jnp.where((lane >= d) & (lane < dv), qf, 0.0)).astype(_BF16)

    prepare_keys_values()
    prepare_queries()
    lp = lam_ref[...]
    lam = (jnp.exp(jnp.sum(lp[0:1] * lp[1:2], axis=-1, keepdims=True))
           - jnp.exp(jnp.sum(lp[2:3] * lp[3:4], axis=-1, keepdims=True)) + lam_init)
    nw = nw_ref[...]
    s_refs = (s0_ref, s1_ref)

    items = []
    for qi in range(nq):
        items += [(qi, 2 * qi, "diag0"), (qi, 2 * qi + 1, "diag1")]
        items += [(qi, t, "full") for t in range(2 * qi)]

    def logits(n):
        qi, t, kind = items[n]
        lo = tq if kind == "diag1" else 0
        q_rows = qs_ref[qi * 2 * tq + lo:(qi + 1) * 2 * tq, :]
        s_refs[n % 2][lo:, :] = _mm_nt(q_rows, kx_ref[t * tk:(t + 1) * tk, :])

    def absorb(n, lo, hi, causal):
        qi, t, _ = items[n]
        st, buf = qi % 2, n % 2
        for r0 in range(lo, hi, _ROW_CHUNK):
            rows = slice(r0, r0 + _ROW_CHUNK)
            s = s_refs[buf][rows, :]
            if causal:
                c0 = (r0 - lo) % hq
                s = s + cm_ref[c0:c0 + _ROW_CHUNK, :]
            m_old = m_ref[st, rows, :]
            m_new = jnp.maximum(m_old, jnp.max(s, axis=-1, keepdims=True))
            p_ref[buf, rows, :] = jnp.exp2(s - jnp.concatenate([m_new] * (tk // _LANE), axis=1)).astype(_BF16)
            al_ref[buf, rows, :] = jnp.exp2(m_old - m_new)
            m_ref[st, rows, :] = m_new
        alpha = al_ref[buf, lo:hi, :]
        acc_ref[st, lo:hi, :] = (jnp.concatenate([alpha, alpha], axis=1) * acc_ref[st, lo:hi, :]
                                 + _mm(p_ref[buf, lo:hi, :], vx_ref[t * tk:(t + 1) * tk, :]))

    def finish(qi):
        st = qi % 2
        for g in range(2):
            r0 = g * tq
            o1 = acc_ref[st, r0:r0 + hq, 0:dv] / acc_ref[st, r0:r0 + hq, dv:]
            o2 = acc_ref[st, r0 + hq:r0 + tq, 0:dv] / acc_ref[st, r0 + hq:r0 + tq, dv:]
            o = o1 - lam * o2
            o = o * lax.rsqrt(jnp.mean(o * o, axis=-1, keepdims=True) + _RMS_EPS) * nw
            o_ref[qi * tq + g * hq:qi * tq + (g + 1) * hq, :] = (o * (1.0 - lam_init)).astype(o_ref.dtype)

    logits(0)
    for n, (qi, t, kind) in enumerate(items):
        if kind == "diag0":
            m_ref[qi % 2] = jnp.full(m_ref.shape[1:], _NEG, _F32)
            acc_ref[qi % 2] = jnp.zeros(acc_ref.shape[1:], _F32)
        if n + 1 < len(items):
            logits(n + 1)
        if kind == "diag0":
            absorb(n, 0, tq, True)
            absorb(n, tq, 2 * tq, False)
        elif kind == "diag1":
            absorb(n, tq, 2 * tq, True)
        else:
            absorb(n, 0, 2 * tq, False)
        if n + 1 == len(items) or items[n + 1][0] != qi:
            finish(qi)


def _diff_attention(hcm, lam_params, norm_w, lam_init, *, batch, seq, heads, blk_q, blk_k, blk_v):
    dv = hcm.shape[2]
    d = dv // 2
    m = hcm.shape[1]
    tq = _tile(seq, 1024)
    return pl.pallas_call(
        functools.partial(_attn_body, heads=heads, tq=tq, d=d, seq=seq),
        grid=(batch, heads),
        in_specs=[pl.BlockSpec(memory_space=pltpu.SMEM),
                  pl.BlockSpec((1, seq, dv), lambda b, h: (blk_q + h, b, 0)),
                  pl.BlockSpec((1, seq, dv), lambda b, h: (blk_k + h, b, 0)),
                  pl.BlockSpec((1, seq, dv), lambda b, h: (blk_v + h, b, 0)),
                  pl.BlockSpec((4, d), lambda b, h: (0, 0)),
                  pl.BlockSpec((1, dv), lambda b, h: (0, 0))],
        out_specs=pl.BlockSpec((seq, dv), lambda b, h: (b, h)),
        out_shape=jax.ShapeDtypeStruct((m, heads * dv), _BF16),
        scratch_shapes=[pltpu.VMEM((seq, 2 * dv), _BF16),
                        pltpu.VMEM((seq, 2 * dv), _BF16),
                        pltpu.VMEM((2 * seq, 2 * dv), _BF16),
                        pltpu.VMEM((2, 2 * tq, _LANE), _F32),
                        pltpu.VMEM((2, 2 * tq, 2 * dv), _F32),
                        pltpu.VMEM((2 * tq, tq // 2), _F32),
                        pltpu.VMEM((2 * tq, tq // 2), _F32),
                        pltpu.VMEM((tq // 2, tq // 2), _F32),
                        pltpu.VMEM((2, 2 * tq, tq // 2), _BF16),
                        pltpu.VMEM((2, 2 * tq, _LANE), _F32)],
        compiler_params=_cparams(("parallel", "parallel")),
        name="diff_attention",
    )(lam_init, hcm, hcm, hcm, lam_params, norm_w)


def _merge_body(oa_ref, ob_ref, wa_ref, wb_ref, ga_ref, gb_ref, o_ref, wa16_ref, wb16_ref):
    @pl.when(pl.program_id(1) == 0)
    def _():
        wa16_ref[...] = wa_ref[...].astype(_BF16)
        wb16_ref[...] = wb_ref[...].astype(_BF16)

    ya = _mm(oa_ref[...], wa16_ref[...])
    yb = _mm(ob_ref[...], wb16_ref[...])
    for j in range(ga_ref.shape[0]):
        sl = slice(j * _LANE, (j + 1) * _LANE)
        o_ref[:, sl] = (_sigmoid(ga_ref[j].astype(_F32)) * ya[:, sl]
                        + _sigmoid(gb_ref[j].astype(_F32)) * yb[:, sl]).astype(o_ref.dtype)


def _merge(oa, ob, wa_all, wb_all, hcm, layer, *, blk_ga, blk_gb):
    m, k = oa.shape
    n = wa_all.shape[2]
    tm, tn = _tile(m, 1024), _tile(n, 512)
    nb = tn // _LANE
    assert blk_ga % nb == 0 and blk_gb % nb == 0
    return pl.pallas_call(
        _merge_body,
        grid=(n // tn, m // tm),
        in_specs=[pl.BlockSpec((tm, k), lambda j, i: (i, 0)),
                  pl.BlockSpec((tm, k), lambda j, i: (i, 0)),
                  pl.BlockSpec((None, k, tn), lambda j, i: (layer, 0, j)),
                  pl.BlockSpec((None, k, tn), lambda j, i: (layer, 0, j)),
                  pl.BlockSpec((nb, tm, _LANE), lambda j, i: (blk_ga // nb + j, i, 0)),
                  pl.BlockSpec((nb, tm, _LANE), lambda j, i: (blk_gb // nb + j, i, 0))],
        out_specs=pl.BlockSpec((tm, tn), lambda j, i: (i, j)),
        out_shape=jax.ShapeDtypeStruct((m, n), _BF16),
        scratch_shapes=[pltpu.VMEM((k, tn), _BF16), pltpu.VMEM((k, tn), _BF16)],
        compiler_params=_cparams(("parallel", "arbitrary")),
        name="branch_merge",
    )(oa, ob, wa_all, wb_all, hcm, hcm)


def _mm_res_ln_body(a_ref, w_ref, x_ref, g_ref, b_ref, of_ref, ob_ref, acc_ref, *, alpha, nk):
    kk = pl.program_id(1)

    half = a_ref.shape[0] // 2

    def finish(first):
        for r in (slice(0, half), slice(half, 2 * half)):
            acc = _mm(a_ref[r, :], w_ref[...])
            if not first:
                acc = acc + acc_ref[r, :]
            y = alpha * x_ref[r, :] + acc
            mu = jnp.mean(y, axis=-1, keepdims=True)
            yc = y - mu
            var = jnp.mean(yc * yc, axis=-1, keepdims=True)
            out = yc * lax.rsqrt(var + _LN_EPS) * g_ref[...] + b_ref[...]
            of_ref[r, :] = out
            ob_ref[r, :] = out.astype(ob_ref.dtype)

    if nk == 1:
        finish(True)
        return

    @pl.when(kk == 0)
    def _():
        acc_ref[...] = _mm(a_ref[...], w_ref[...])

    @pl.when((kk > 0) & (kk < nk - 1))
    def _():
        acc_ref[...] += _mm(a_ref[...], w_ref[...])

    @pl.when(kk == nk - 1)
    def _():
        finish(False)


def _mm_res_ln(a, w_all, x, ln_g, ln_b, layer, which, alpha, tk_pref):
    m, k = a.shape
    n = w_all.shape[2]
    tm, tk = _tile(m, 512), _tile(k, tk_pref)
    nk = k // tk
    return pl.pallas_call(
        functools.partial(_mm_res_ln_body, alpha=alpha, nk=nk),
        grid=(m // tm, nk),
        in_specs=[pl.BlockSpec((tm, tk), lambda i, j: (i, j)),
                  pl.BlockSpec((None, tk, n), lambda i, j: (layer, j, 0)),
                  pl.BlockSpec((tm, n), lambda i, j: (i, 0)),
                  pl.BlockSpec((None, 1, n), lambda i, j: (2 * layer + which, 0, 0)),
                  pl.BlockSpec((None, 1, n), lambda i, j: (2 * layer + which, 0, 0))],
        out_specs=[pl.BlockSpec((tm, n), lambda i, j: (i, 0)),
                   pl.BlockSpec((tm, n), lambda i, j: (i, 0))],
        out_shape=[jax.ShapeDtypeStruct((m, n), _F32), jax.ShapeDtypeStruct((m, n), _BF16)],
        scratch_shapes=[pltpu.VMEM((tm, n) if nk > 1 else (_SUBLANE, _LANE), _F32)],
        compiler_params=_cparams(("parallel", "arbitrary")),
        name="matmul_residual_layernorm",
    )(a, w_all, x, ln_g, ln_b)


def _swiglu_body(x_ref, wg_ref, wu_ref, o_ref, wg16_ref, wu16_ref):
    @pl.when(pl.program_id(1) == 0)
    def _():
        wg16_ref[...] = wg_ref[...].astype(_BF16)
        wu16_ref[...] = wu_ref[...].astype(_BF16)

    x = x_ref[...]
    g = _mm(x, wg16_ref[...])
    u = _mm(x, wu16_ref[...])
    o_ref[...] = (g * _sigmoid(g) * u).astype(o_ref.dtype)


def _swiglu(xb, wg_all, wu_all, layer):
    m, k = xb.shape
    n = wg_all.shape[2]
    tm, tn = _tile(m, 1024), _tile(n, 512)
    return pl.pallas_call(
        _swiglu_body,
        grid=(n // tn, m // tm),
        in_specs=[pl.BlockSpec((tm, k), lambda j, i: (i, 0)),
                  pl.BlockSpec((None, k, tn), lambda j, i: (layer, 0, j)),
                  pl.BlockSpec((None, k, tn), lambda j, i: (layer, 0, j))],
        out_specs=pl.BlockSpec((tm, tn), lambda j, i: (i, j)),
        out_shape=jax.ShapeDtypeStruct((m, n), _BF16),
        scratch_shapes=[pltpu.VMEM((k, tn), _BF16), pltpu.VMEM((k, tn), _BF16)],
        compiler_params=_cparams(("parallel", "arbitrary")),
        name="swiglu_gate_up",
    )(xb, wg_all, wu_all)


def kernel(x, w_in, conv_w, a_log, dt_bias, dn_norm_w, da_lambda, da_norm_w, w_branch_a, w_branch_b,
           w_out, ln_g, ln_b, w_gate, w_up, w_down):
    batch, seq, d_model = x.shape
    depth = w_in.shape[0]
    dn_heads = a_log.shape[1]
    dk = dn_norm_w.shape[1]
    d = da_lambda.shape[2]
    dv = da_norm_w.shape[1]
    da_heads = w_branch_b.shape[1] // dv
    dn_qk = dn_heads * dk
    da_qk = da_heads * 2 * d
    assert dk == _LANE and dv == _LANE and 2 * d == dv and dn_heads % 2 == 0
    assert w_in.shape[2] == 4 * dn_qk + 2 * dn_heads + 2 * da_qk + da_heads * dv + 2 * d_model
    m = batch * seq
    alpha = (2 * depth) ** 0.25

    small = slice(4 * dn_qk, 4 * dn_qk + 2 * dn_heads)
    n_rest = w_in.shape[2] - small.stop
    w_rows = jnp.swapaxes(w_in[:, :, small], 1, 2).astype(_BF16)
    wo, wd = w_out.astype(_BF16), w_down.astype(_BF16)
    ln_g2 = ln_g.reshape(2 * depth, 1, d_model)
    ln_b2 = ln_b.reshape(2 * depth, 1, d_model)

    nb_dn, nb_da, nb_model = dn_qk // _LANE, da_qk // _LANE, d_model // _LANE
    blk_aq, blk_ak, blk_av = 0, nb_da, 2 * nb_da
    blk_ga = blk_av + da_heads
    blk_gb = blk_ga + nb_model

    xf = x.reshape(m, d_model)
    xb = xf.astype(_BF16)
    for layer in range(depth):
        h_dn = _project(xb, w_in, layer, 0, small.start)
        h_rest = _project(xb, w_in, layer, small.stop, n_rest)
        rows = _project_rows(xb, w_rows, layer).reshape(2 * dn_heads, 1, m)
        conv_t = jnp.swapaxes(conv_w[layer].reshape(_CONV_TAPS, 3 * dn_heads, dk), 0, 1)
        ab = jnp.stack([a_log[layer], dt_bias[layer]])
        o_a = _deltanet(h_dn, rows, conv_t, ab, dn_norm_w[layer].reshape(1, dk),
                        batch=batch, seq=seq, heads=dn_heads,
                        blk_q=0, blk_k=nb_dn, blk_v=2 * nb_dn, blk_z=3 * nb_dn)
        lam_init = jnp.full((1,), 0.8 - 0.6 * math.exp(-0.3 * layer), _F32)
        o_b = _diff_attention(h_rest, da_lambda[layer], da_norm_w[layer].reshape(1, dv), lam_init,
                              batch=batch, seq=seq, heads=da_heads,
                              blk_q=blk_aq, blk_k=blk_ak, blk_v=blk_av)
        merged = _merge(o_a, o_b, w_branch_a, w_branch_b, h_rest, layer, blk_ga=blk_ga, blk_gb=blk_gb)
        xf, xb = _mm_res_ln(merged, wo, xf, ln_g2, ln_b2, layer, 0, alpha, 2048)
        act = _swiglu(xb, w_gate, w_up, layer)
        xf, xb = _mm_res_ln(act, wd, xf, ln_g2, ln_b2, layer, 1, alpha, 1408)
    return xf.reshape(batch, seq, d_model)
```

```python
import functools
import math

import jax
import jax.numpy as jnp
from jax import lax
from jax.experimental import pallas as pl
from jax.experimental.pallas import tpu as pltpu

_F32 = jnp.float32
_BF16 = jnp.bfloat16
_LANE = 128
_SUBLANE = 8
_VMEM_LIMIT = 56 * 1024 * 1024
_CHUNK = 128
_INV_BASE = 16
_CONV_TAPS = 4
_LN_EPS = 1e-5
_RMS_EPS = 1e-6
_NEG = -1e30
_LOG2E = 1.4426950408889634
_BIAS_COLS = 3
_ROW_CHUNK = 64

_NT = (((1,), (1,)), ((), ()))


def _mm(a, b):
    return jnp.dot(a, b, preferred_element_type=_F32)


def _mm_nt(a, b):
    return lax.dot_general(a, b, _NT, preferred_element_type=_F32)


def _cparams(semantics):
    return pltpu.CompilerParams(dimension_semantics=semantics, vmem_limit_bytes=_VMEM_LIMIT)


def _sigmoid(x):
    return 1.0 / (1.0 + jnp.exp(-x))


def _softplus(x):
    return jnp.maximum(x, 0.0) + jnp.log(1.0 + jnp.exp(-jnp.abs(x)))


def _tile(n, pref):
    t = min(n, pref)
    while n % t:
        t -= _LANE
    return t


def _round_robin(chains):
    chains = list(chains)
    while chains:
        alive = []
        for g in chains:
            try:
                next(g)
                alive.append(g)
            except StopIteration:
                pass
        chains = alive


def _proj_body(*refs, shift):
    if shift:
        x_ref, w_ref, wnext_ref, o_ref, wb_ref = refs
    else:
        x_ref, w_ref, o_ref, wb_ref = refs

    @pl.when(pl.program_id(1) == 0)
    def _():
        w = w_ref[...]
        if shift:
            w = jnp.concatenate([w[shift:, :], wnext_ref[:shift, :]], axis=0)
        wb_ref[...] = w.T.astype(_BF16)

    acc = _mm(x_ref[...], wb_ref[...])
    for j in range(o_ref.shape[0]):
        o_ref[j] = acc[:, j * _LANE:(j + 1) * _LANE].astype(o_ref.dtype)


def _project(xb, wt_all, layer, col0, n):
    m, k = xb.shape
    tm, tn = _tile(m, 2048), _tile(n, 512)
    shift = col0 % tn
    assert shift % _SUBLANE == 0
    blk0 = col0 // tn
    w_specs = [pl.BlockSpec((None, tn, k), lambda j, i: (layer, blk0 + j, 0))]
    if shift:
        w_specs.append(pl.BlockSpec((None, tn, k), lambda j, i: (layer, blk0 + j + 1, 0)))
    return pl.pallas_call(
        functools.partial(_proj_body, shift=shift),
        grid=(n // tn, m // tm),
        in_specs=[pl.BlockSpec((tm, k), lambda j, i: (i, 0))] + w_specs,
        out_specs=pl.BlockSpec((tn // _LANE, tm, _LANE), lambda j, i: (j, i, 0)),
        out_shape=jax.ShapeDtypeStruct((n // _LANE, m, _LANE), _BF16),
        scratch_shapes=[pltpu.VMEM((k, tn), _BF16)],
        compiler_params=_cparams(("parallel", "arbitrary")),
        name="in_proj",
    )(xb, *([wt_all] * len(w_specs)))


def _rowproj_body(w_ref, x_ref, o_ref):
    o_ref[...] = _mm_nt(w_ref[...], x_ref[...])


def _project_rows(xb, wt_rows, layer):
    m, k = xb.shape
    r = wt_rows.shape[1]
    tm = _tile(m, 1024)
    return pl.pallas_call(
        _rowproj_body,
        grid=(m // tm,),
        in_specs=[pl.BlockSpec((None, r, k), lambda i: (layer, 0, 0)),
                  pl.BlockSpec((tm, k), lambda i: (i, 0))],
        out_specs=pl.BlockSpec((r, tm), lambda i: (0, i)),
        out_shape=jax.ShapeDtypeStruct((r, m), _F32),
        compiler_params=_cparams(("parallel",)),
        name="gate_logits",
    )(wt_rows, xb)


def _dn_body(ab_ref, q_ref, k_ref, v_ref, z_ref, cq_ref, ck_ref, cv_ref, bl_ref, al_ref, nw_ref,
             o_ref, s_ref, tail_ref, xb_ref, *, hb, tl, dk):
    hg = pl.program_id(1)
    t = pl.program_id(2)
    c = _CHUNK
    nchunk = tl // c

    @pl.when(t == 0)
    def _():
        s_ref[...] = jnp.zeros_like(s_ref)
        tail_ref[...] = jnp.zeros_like(tail_ref)

    row = lax.broadcasted_iota(jnp.int32, (c, c), 0)
    col = lax.broadcasted_iota(jnp.int32, (c, c), 1)
    eye = row == col
    lower = row >= col
    eye_f = jnp.where(eye, 1.0, 0.0).astype(_F32)
    widths = []
    w_ = _INV_BASE
    while w_ < c:
        widths.append(w_)
        w_ *= 2
    same_blk = [(row // w_) == (col // w_) for w_ in widths]
    pos = lax.broadcasted_iota(jnp.int32, (_SUBLANE, tl), 1) % c

    def conv_silu(x_ref, c_ref, hh, idx):
        x = x_ref[hh].astype(_F32)
        xb_ref[idx, 0:_SUBLANE, :] = tail_ref[idx]
        xb_ref[idx, _SUBLANE:, :] = x
        tail_ref[idx] = x[tl - _SUBLANE:, :]
        w = c_ref[hh]
        y = w[_CONV_TAPS - 1:_CONV_TAPS, :] * x
        for j in range(_CONV_TAPS - 1):
            off = _SUBLANE - (_CONV_TAPS - 1) + j
            y = y + w[j:j + 1, :] * xb_ref[idx, off:off + tl, :]
        return y * _sigmoid(y)

    def l2n(x):
        return x * lax.rsqrt(jnp.sum(x * x, axis=-1, keepdims=True) + _RMS_EPS)

    def to_col(r):
        return jnp.sum(jnp.where(eye, r, 0.0), axis=-1, keepdims=True)

    heads = []
    for hh in range(hb):
        h = hg * hb + hh
        q = l2n(conv_silu(q_ref, cq_ref, hh, 3 * hh)) * (dk ** -0.5)
        k = l2n(conv_silu(k_ref, ck_ref, hh, 3 * hh + 1))
        v = conv_silu(v_ref, cv_ref, hh, 3 * hh + 2)
        beta_r = _sigmoid(bl_ref[hh])
        a_exp = jnp.exp(jnp.full((1, tl), ab_ref[0, h], _F32))
        g_r = -a_exp * _softplus(al_ref[hh] + ab_ref[1, h])
        g8 = jnp.broadcast_to(g_r, (_SUBLANE, tl))
        gc = g8
        sf = g8
        s_ = 1
        while s_ < c:
            gc = gc + jnp.where(pos >= s_, pltpu.roll(gc, s_, axis=1), 0.0)
            sf = sf + jnp.where(pos < c - s_, pltpu.roll(sf, tl - s_, axis=1), 0.0)
            s_ *= 2
        heads.append(dict(q=q, k=k, v=v, beta_r=beta_r, gc_r=gc[0:1, :],
                          tail_r=jnp.exp(sf[0:1, :] - g_r)))

    blocks = {}

    def chunk_chain(hh, ci):
        hd = heads[hh]
        lo, hi = ci * c, (ci + 1) * c
        qp, kp, vp = hd["q"][lo:hi], hd["k"][lo:hi], hd["v"][lo:hi]
        gc_rp = hd["gc_r"][:, lo:hi]
        beta_c, gc_c = to_col(hd["beta_r"][:, lo:hi]), to_col(gc_rp)
        egc_c = jnp.exp(gc_c)
        kpb = kp.astype(_BF16)
        qk = _mm_nt(jnp.concatenate([qp.astype(_BF16), kpb], axis=0), kpb)
        yield
        dm = jnp.exp(jnp.where(lower, gc_c - gc_rp, _NEG))
        a = jnp.where(eye, 0.0, qk[c:] * beta_c * dm)
        qd = (qk[:c] * dm).astype(_BF16)
        ad = jnp.where(same_blk[0], a, 0.0)
        x = eye_f - ad
        ap = ad.astype(_BF16)
        w_ = 2
        while w_ < _INV_BASE:
            ap = _mm(ap, ap).astype(_BF16)
            yield
            x = x + _mm(x.astype(_BF16), ap)
            yield
            w_ *= 2
        for lvl in range(len(widths)):
            inner = same_blk[lvl]
            off = (a if lvl + 1 == len(widths) else jnp.where(same_blk[lvl + 1], a, 0.0))
            off = jnp.where(inner, 0.0, off).astype(_BF16)
            xb = x.astype(_BF16)
            t1 = _mm(xb, off).astype(_BF16)
            yield
            x = x - _mm(t1, xb)
            yield
        rhs = jnp.concatenate([vp * beta_c, kp * (beta_c * egc_c)], axis=1).astype(_BF16)
        sol = _mm(x.astype(_BF16), rhs)
        yield
        blocks[(hh, ci)] = dict(
            u=sol[:, :dk], qd=qd,
            wq=jnp.concatenate([sol[:, dk:].astype(_BF16), (qp * egc_c).astype(_BF16)], axis=0),
            kt=(kp.T * hd["tail_r"][:, lo:hi]).astype(_BF16),
            gt=egc_c[c - 1:c, :])

    nw = nw_ref[...]

    def state_chain(hh):
        s = s_ref[hh]
        for ci in range(nchunk):
            b = blocks[(hh, ci)]
            r = _mm(b["wq"], s.astype(_BF16))
            yield
            vn = (b["u"] - r[:c]).astype(_BF16)
            s = s * b["gt"] + _mm(b["kt"], vn)
            yield
            o = r[c:] + _mm(b["qd"], vn)
            yield
            lo, hi = ci * c, (ci + 1) * c
            zf = z_ref[hh, lo:hi, :].astype(_F32)
            o = o * lax.rsqrt(jnp.mean(o * o, axis=-1, keepdims=True) + _RMS_EPS) * nw
            o_ref[lo:hi, hh * dk:(hh + 1) * dk] = (o * (zf * _sigmoid(zf))).astype(o_ref.dtype)
        s_ref[hh] = s

    _round_robin([chunk_chain(hh, ci) for ci in range(nchunk) for hh in range(hb)])
    _round_robin([state_chain(hh) for hh in range(hb)])


def _deltanet(hcm, rows, conv_t, ab, norm_w, *, batch, seq, heads, blk_q, blk_k, blk_v, blk_z):
    dk = hcm.shape[2]
    m = hcm.shape[1]
    hb = 4 if heads % 4 == 0 else 2
    tl = _tile(seq, 512)
    nt = seq // tl

    def act(off):
        return pl.BlockSpec((hb, tl, dk), lambda b, g, t: (off // hb + g, b * nt + t, 0))

    def taps(off):
        return pl.BlockSpec((hb, _CONV_TAPS, dk), lambda b, g, t: (off // hb + g, 0, 0))

    def logit(off):
        return pl.BlockSpec((hb, 1, tl), lambda b, g, t: (off // hb + g, 0, b * nt + t))

    return pl.pallas_call(
        functools.partial(_dn_body, hb=hb, tl=tl, dk=dk),
        grid=(batch, heads // hb, nt),
        in_specs=[pl.BlockSpec(memory_space=pltpu.SMEM),
                  act(blk_q), act(blk_k), act(blk_v), act(blk_z),
                  taps(0), taps(heads), taps(2 * heads),
                  logit(0), logit(heads),
                  pl.BlockSpec((1, dk), lambda b, g, t: (0, 0))],
        out_specs=pl.BlockSpec((tl, hb * dk), lambda b, g, t: (b * nt + t, g)),
        out_shape=jax.ShapeDtypeStruct((m, heads * dk), _BF16),
        scratch_shapes=[pltpu.VMEM((hb, dk, dk), _F32),
                        pltpu.VMEM((3 * hb, _SUBLANE, dk), _F32),
                        pltpu.VMEM((3 * hb, tl + _SUBLANE, dk), _F32)],
        compiler_params=_cparams(("parallel", "parallel", "arbitrary")),
        name="gated_deltanet",
    )(ab, hcm, hcm, hcm, hcm, conv_t, conv_t, conv_t, rows, rows, norm_w)


def _attn_body(li_ref, q_ref, k_ref, v_ref, lam_ref, nw_ref, o_ref, kx_ref, vx_ref, qs_ref, m_ref, acc_ref,
               s0_ref, s1_ref, cm_ref, p_ref, al_ref, *, heads, tq, d, seq):
    h = pl.program_id(1)
    hq = tq // 2
    tk = hq
    nq = seq // tq
    dv = 2 * d
    kw = 2 * dv
    lam_init = li_ref[0]

    def prepare_keys_values():
        hf = (h + 1).astype(_F32)
        slope2 = jnp.exp2(jnp.full((1, dv), -8.0 / heads, _F32) * hf) * _LOG2E
        rows = 4 * _ROW_CHUNK
        lane = lax.broadcasted_iota(jnp.int32, (rows, dv), 1)
        jloc = lax.broadcasted_iota(jnp.int32, (rows, dv), 0).astype(_F32)
        for r0 in range(0, seq, rows):
            rest = (jloc + float(r0 - seq)) * slope2
            ext = jnp.zeros((rows, dv), _F32)
            for piece in range(_BIAS_COLS):
                part = rest.astype(_BF16).astype(_F32)
                rest = rest - part
                ext = jnp.where(lane == piece, part, ext)
            kx_ref[r0:r0 + rows, dv:] = ext.astype(_BF16)
        kx_ref[:, :dv] = k_ref[0]
        vx_ref[:, :dv] = v_ref[0]
        vx_ref[:, dv:] = jnp.ones((seq, dv), _BF16)
        rpos = lax.broadcasted_iota(jnp.int32, (hq, tk), 0)
        cpos = lax.broadcasted_iota(jnp.int32, (hq, tk), 1)
        cm_ref[...] = jnp.where(rpos >= cpos, 0.0, _NEG).astype(_F32)

    def prepare_queries():
        lane = lax.broadcasted_iota(jnp.int32, (hq, kw), 1)
        ones = (lane >= dv) & (lane < dv + _BIAS_COLS)
        for g in range(2 * nq):
            qf = q_ref[0, g * hq:(g + 1) * hq, :].astype(_F32) * (d ** -0.5 * _LOG2E)
            qf = jnp.concatenate([qf, jnp.zeros((hq, dv), _F32)], axis=1)
            qs_ref[g * tq:g * tq + hq, :] = jnp.where(ones, 1.0, jnp.where(lane < d, qf, 0.0)).astype(_BF16)
            qs_ref[g * tq + hq:(g + 1) * tq, :] = jnp.where(
                ones, 1.0, jnp.where((lane >= d) & (lane < dv), qf, 0.0)).astype(_BF16)

    prepare_keys_values()
    prepare_queries()
    lp = lam_ref[...]
    lam = (jnp.exp(jnp.sum(lp[0:1] * lp[1:2], axis=-1, keepdims=True))
           - jnp.exp(jnp.sum(lp[2:3] * lp[3:4], axis=-1, keepdims=True)) + lam_init)
    nw = nw_ref[...]
    s_refs = (s0_ref, s1_ref)

    items = []
    for qi in range(nq):
        items += [(qi, 2 * qi, "diag0"), (qi, 2 * qi + 1, "diag1")]
        items += [(qi, t, "full") for t in range(2 * qi)]

    def logits(n):
        qi, t, kind = items[n]
        lo = tq if kind == "diag1" else 0
        q_rows = qs_ref[qi * 2 * tq + lo:(qi + 1) * 2 * tq, :]
        s_refs[n % 2][lo:, :] = _mm_nt(q_rows, kx_ref[t * tk:(t + 1) * tk, :])

    def absorb(n, lo, hi, causal):
        qi, t, _ = items[n]
        st, buf = qi % 2, n % 2
        for r0 in range(lo, hi, _ROW_CHUNK):
            rows = slice(r0, r0 + _ROW_CHUNK)
            s = s_refs[buf][rows, :]
            if causal:
                c0 = (r0 - lo) % hq
                s = s + cm_ref[c0:c0 + _ROW_CHUNK, :]
            m_old = m_ref[st, rows, :]
            m_new = jnp.maximum(m_old, jnp.max(s, axis=-1, keepdims=True))
            p_ref[buf, rows, :] = jnp.exp2(s - jnp.concatenate([m_new] * (tk // _LANE), axis=1)).astype(_BF16)
            al_ref[buf, rows, :] = jnp.exp2(m_old - m_new)
            m_ref[st, rows, :] = m_new
        alpha = al_ref[buf, lo:hi, :]
        acc_ref[st, lo:hi, :] = (jnp.concatenate([alpha, alpha], axis=1) * acc_ref[st, lo:hi, :]
                                 + _mm(p_ref[buf, lo:hi, :], vx_ref[t * tk:(t + 1) * tk, :]))

    def finish(qi):
        st = qi % 2
        for g in range(2):
            r0 = g * tq
            o1 = acc_ref[st, r0:r0 + hq, 0:dv] / acc_ref[st, r0:r0 + hq, dv:]
            o2 = acc_ref[st, r0 + hq:r0 + tq, 0:dv] / acc_ref[st, r0 + hq:r0 + tq, dv:]
            o = o1 - lam * o2
            o = o * lax.rsqrt(jnp.mean(o * o, axis=-1, keepdims=True) + _RMS_EPS) * nw
            o_ref[qi * tq + g * hq:qi * tq + (g + 1) * hq, :] = (o * (1.0 - lam_init)).astype(o_ref.dtype)

    logits(0)
    for n, (qi, t, kind) in enumerate(items):
        if kind == "diag0":
            m_ref[qi % 2] = jnp.full(m_ref.shape[1:], _NEG, _F32)
            acc_ref[qi % 2] = jnp.zeros(acc_ref.shape[1:], _F32)
        if n + 1 < len(items):
            logits(n + 1)
        if kind == "diag0":
            absorb(n, 0, tq, True)
            absorb(n, tq, 2 * tq, False)
        elif kind == "diag1":
            absorb(n, tq, 2 * tq, True)
        else:
            absorb(n, 0, 2 * tq, False)
        if n + 1 == len(items) or items[n + 1][0] != qi:
            finish(qi)


def _diff_attention(hcm, lam_params, norm_w, lam_init, *, batch, seq, heads, blk_q, blk_k, blk_v):
    dv = hcm.shape[2]
    d = dv // 2
    m = hcm.shape[1]
    tq = _tile(seq, 1024)
    return pl.pallas_call(
        functools.partial(_attn_body, heads=heads, tq=tq, d=d, seq=seq),
        grid=(batch, heads),
        in_specs=[pl.BlockSpec(memory_space=pltpu.SMEM),
                  pl.BlockSpec((1, seq, dv), lambda b, h: (blk_q + h, b, 0)),
                  pl.BlockSpec((1, seq, dv), lambda b, h: (blk_k + h, b, 0)),
                  pl.BlockSpec((1, seq, dv), lambda b, h: (blk_v + h, b, 0)),
                  pl.BlockSpec((4, d), lambda b, h: (0, 0)),
                  pl.BlockSpec((1, dv), lambda b, h: (0, 0))],
        out_specs=pl.BlockSpec((seq, dv), lambda b, h: (b, h)),
        out_shape=jax.ShapeDtypeStruct((m, heads * dv), _BF16),
        scratch_shapes=[pltpu.VMEM((seq, 2 * dv), _BF16),
                        pltpu.VMEM((seq, 2 * dv), _BF16),
                        pltpu.VMEM((2 * seq, 2 * dv), _BF16),
                        pltpu.VMEM((2, 2 * tq, _LANE), _F32),
                        pltpu.VMEM((2, 2 * tq, 2 * dv), _F32),
                        pltpu.VMEM((2 * tq, tq // 2), _F32),
                        pltpu.VMEM((2 * tq, tq // 2), _F32),
                        pltpu.VMEM((tq // 2, tq // 2), _F32),
                        pltpu.VMEM((2, 2 * tq, tq // 2), _BF16),
                        pltpu.VMEM((2, 2 * tq, _LANE), _F32)],
        compiler_params=_cparams(("parallel", "parallel")),
        name="diff_attention",
    )(lam_init, hcm, hcm, hcm, lam_params, norm_w)


def _merge_body(oa_ref, ob_ref, wa_ref, wb_ref, ga_ref, gb_ref, o_ref, wa16_ref, wb16_ref):
    @pl.when(pl.program_id(1) == 0)
    def _():
        wa16_ref[...] = wa_ref[...].astype(_BF16)
        wb16_ref[...] = wb_ref[...].astype(_BF16)

    ya = _mm(oa_ref[...], wa16_ref[...])
    yb = _mm(ob_ref[...], wb16_ref[...])
    for j in range(ga_ref.shape[0]):
        sl = slice(j * _LANE, (j + 1) * _LANE)
        o_ref[:, sl] = (_sigmoid(ga_ref[j].astype(_F32)) * ya[:, sl]
                        + _sigmoid(gb_ref[j].astype(_F32)) * yb[:, sl]).astype(o_ref.dtype)


def _merge(oa, ob, wa_all, wb_all, hcm, layer, *, blk_ga, blk_gb):
    m, k = oa.shape
    n = wa_all.shape[2]
    tm, tn = _tile(m, 1024), _tile(n, 512)
    nb = tn // _LANE
    assert blk_ga % nb == 0 and blk_gb % nb == 0
    return pl.pallas_call(
        _merge_body,
        grid=(n // tn, m // tm),
        in_specs=[pl.BlockSpec((tm, k), lambda j, i: (i, 0)),
                  pl.BlockSpec((tm, k), lambda j, i: (i, 0)),
                  pl.BlockSpec((None, k, tn), lambda j, i: (layer, 0, j)),
                  pl.BlockSpec((None, k, tn), lambda j, i: (layer, 0, j)),
                  pl.BlockSpec((nb, tm, _LANE), lambda j, i: (blk_ga // nb + j, i, 0)),
                  pl.BlockSpec((nb, tm, _LANE), lambda j, i: (blk_gb // nb + j, i, 0))],
        out_specs=pl.BlockSpec((tm, tn), lambda j, i: (i, j)),
        out_shape=jax.ShapeDtypeStruct((m, n), _BF16),
        scratch_shapes=[pltpu.VMEM((k, tn), _BF16), pltpu.VMEM((k, tn), _BF16)],
        compiler_params=_cparams(("parallel", "arbitrary")),
        name="branch_merge",
    )(oa, ob, wa_all, wb_all, hcm, hcm)


def _mm_res_ln_body(a_ref, w_ref, x_ref, g_ref, b_ref, of_ref, ob_ref, acc_ref, *, alpha, nk):
    kk = pl.program_id(1)

    half = a_ref.shape[0] // 2

    def finish(first):
        for r in (slice(0, half), slice(half, 2 * half)):
            acc = _mm(a_ref[r, :], w_ref[...])
            if not first:
                acc = acc + acc_ref[r, :]
            y = alpha * x_ref[r, :] + acc
            mu = jnp.mean(y, axis=-1, keepdims=True)
            yc = y - mu
            var = jnp.mean(yc * yc, axis=-1, keepdims=True)
            out = yc * lax.rsqrt(var + _LN_EPS) * g_ref[...] + b_ref[...]
            of_ref[r, :] = out
            ob_ref[r, :] = out.astype(ob_ref.dtype)

    if nk == 1:
        finish(True)
        return

    @pl.when(kk == 0)
    def _():
        acc_ref[...] = _mm(a_ref[...], w_ref[...])

    @pl.when((kk > 0) & (kk < nk - 1))
    def _():
        acc_ref[...] += _mm(a_ref[...], w_ref[...])

    @pl.when(kk == nk - 1)
    def _():
        finish(False)


def _mm_res_ln(a, w_all, x, ln_g, ln_b, layer, which, alpha, tk_pref):
    m, k = a.shape
    n = w_all.shape[2]
    tm, tk = _tile(m, 512), _tile(k, tk_pref)
    nk = k // tk
    return pl.pallas_call(
        functools.partial(_mm_res_ln_body, alpha=alpha, nk=nk),
        grid=(m // tm, nk),
        in_specs=[pl.BlockSpec((tm, tk), lambda i, j: (i, j)),
                  pl.BlockSpec((None, tk, n), lambda i, j: (layer, j, 0)),
                  pl.BlockSpec((tm, n), lambda i, j: (i, 0)),
                  pl.BlockSpec((None, 1, n), lambda i, j: (2 * layer + which, 0, 0)),
                  pl.BlockSpec((None, 1, n), lambda i, j: (2 * layer + which, 0, 0))],
        out_specs=[pl.BlockSpec((tm, n), lambda i, j: (i, 0)),
                   pl.BlockSpec((tm, n), lambda i, j: (i, 0))],
        out_shape=[jax.ShapeDtypeStruct((m, n), _F32), jax.ShapeDtypeStruct((m, n), _BF16)],
        scratch_shapes=[pltpu.VMEM((tm, n) if nk > 1 else (_SUBLANE, _LANE), _F32)],
        compiler_params=_cparams(("parallel", "arbitrary")),
        name="matmul_residual_layernorm",
    )(a, w_all, x, ln_g, ln_b)


def _swiglu_body(x_ref, wg_ref, wu_ref, o_ref, wg16_ref, wu16_ref):
    @pl.when(pl.program_id(1) == 0)
    def _():
        wg16_ref[...] = wg_ref[...].astype(_BF16)
        wu16_ref[...] = wu_ref[...].astype(_BF16)

    x = x_ref[...]
    g = _mm(x, wg16_ref[...])
    u = _mm(x, wu16_ref[...])
    o_ref[...] = (g * _sigmoid(g) * u).astype(o_ref.dtype)


def _swiglu(xb, wg_all, wu_all, layer):
    m, k = xb.shape
    n = wg_all.shape[2]
    tm, tn = _tile(m, 1024), _tile(n, 512)
    return pl.pallas_call(
        _swiglu_body,
        grid=(n // tn, m // tm),
        in_specs=[pl.BlockSpec((tm, k), lambda j, i: (i, 0)),
                  pl.BlockSpec((None, k, tn), lambda j, i: (layer, 0, j)),
                  pl.BlockSpec((None, k, tn), lambda j, i: (layer, 0, j))],
        out_specs=pl.BlockSpec((tm, tn), lambda j, i: (i, j)),
        out_shape=jax.ShapeDtypeStruct((m, n), _BF16),
        scratch_shapes=[pltpu.VMEM((k, tn), _BF16), pltpu.VMEM((k, tn), _BF16)],
        compiler_params=_cparams(("parallel", "arbitrary")),
        name="swiglu_gate_up",
    )(xb, wg_all, wu_all)


def kernel(x, w_in, conv_w, a_log, dt_bias, dn_norm_w, da_lambda, da_norm_w, w_branch_a, w_branch_b,
           w_out, ln_g, ln_b, w_gate, w_up, w_down):
    batch, seq, d_model = x.shape
    depth = w_in.shape[0]
    dn_heads = a_log.shape[1]
    dk = dn_norm_w.shape[1]
    d = da_lambda.shape[2]
    dv = da_norm_w.shape[1]
    da_heads = w_branch_b.shape[1] // dv
    dn_qk = dn_heads * dk
    da_qk = da_heads * 2 * d
    assert dk == _LANE and dv == _LANE and 2 * d == dv and dn_heads % 2 == 0
    assert w_in.shape[2] == 4 * dn_qk + 2 * dn_heads + 2 * da_qk + da_heads * dv + 2 * d_model
    m = batch * seq
    alpha = (2 * depth) ** 0.25

    small = slice(4 * dn_qk, 4 * dn_qk + 2 * dn_heads)
    n_rest = w_in.shape[2] - small.stop
    assert (2 * dn_heads) % _SUBLANE == 0
    w_in_t = jnp.swapaxes(w_in, 1, 2)
    w_rows = w_in_t[:, small, :].astype(_BF16)
    wo, wd = w_out.astype(_BF16), w_down.astype(_BF16)
    ln_g2 = ln_g.reshape(2 * depth, 1, d_model)
    ln_b2 = ln_b.reshape(2 * depth, 1, d_model)

    nb_dn, nb_da, nb_model = dn_qk // _LANE, da_qk // _LANE, d_model // _LANE
    blk_aq, blk_ak, blk_av = 0, nb_da, 2 * nb_da
    blk_ga = blk_av + da_heads
    blk_gb = blk_ga + nb_model

    xf = x.reshape(m, d_model)
    xb = xf.astype(_BF16)
    for layer in range(depth):
        h_dn = _project(xb, w_in_t, layer, 0, small.start)
        h_rest = _project(xb, w_in_t, layer, small.stop, n_rest)
        rows = _project_rows(xb, w_rows, layer).reshape(2 * dn_heads, 1, m)
        conv_t = jnp.swapaxes(conv_w[layer].reshape(_CONV_TAPS, 3 * dn_heads, dk), 0, 1)
        ab = jnp.stack([a_log[layer], dt_bias[layer]])
        o_a = _deltanet(h_dn, rows, conv_t, ab, dn_norm_w[layer].reshape(1, dk),
                        batch=batch, seq=seq, heads=dn_heads,
                        blk_q=0, blk_k=nb_dn, blk_v=2 * nb_dn, blk_z=3 * nb_dn)
        lam_init = jnp.full((1,), 0.8 - 0.6 * math.exp(-0.3 * layer), _F32)
        o_b = _diff_attention(h_rest, da_lambda[layer], da_norm_w[layer].reshape(1, dv), lam_init,
                              batch=batch, seq=seq, heads=da_heads,
                              blk_q=blk_aq, blk_k=blk_ak, blk_v=blk_av)
        merged = _merge(o_a, o_b, w_branch_a, w_branch_b, h_rest, layer, blk_ga=blk_ga, blk_gb=blk_gb)
        xf, xb = _mm_res_ln(merged, wo, xf, ln_g2, ln_b2, layer, 0, alpha, 2048)
        act = _swiglu(xb, w_gate, w_up, layer)
        xf, xb = _mm_res_ln(act, wd, xf, ln_g2, ln_b2, layer, 1, alpha, 1408)
    return xf.reshape(batch, seq, d_model)
```

```python
import functools
import math

import jax
import jax.numpy as jnp
from jax import lax
from jax.experimental import pallas as pl
from jax.experimental.pallas import tpu as pltpu

_F32 = jnp.float32
_BF16 = jnp.bfloat16
_LANE = 128
_SUBLANE = 8
_VMEM_LIMIT = 56 * 1024 * 1024
_CHUNK = 128
_INV_BASE = 2
_CONV_TAPS = 4
_LN_EPS = 1e-5
_RMS_EPS = 1e-6
_NEG = -1e30
_LOG2E = 1.4426950408889634
_BIAS_COLS = 3
_ROW_CHUNK = 64

_NT = (((1,), (1,)), ((), ()))


def _mm(a, b):
    return jnp.dot(a, b, preferred_element_type=_F32)


def _mm_nt(a, b):
    return lax.dot_general(a, b, _NT, preferred_element_type=_F32)


def _cparams(semantics):
    return pltpu.CompilerParams(dimension_semantics=semantics, vmem_limit_bytes=_VMEM_LIMIT)


def _sigmoid(x):
    return 1.0 / (1.0 + jnp.exp(-x))


def _softplus(x):
    return jnp.maximum(x, 0.0) + jnp.log(1.0 + jnp.exp(-jnp.abs(x)))


def _tile(n, pref):
    t = min(n, pref)
    while n % t:
        t -= _LANE
    return t


def _round_robin(chains):
    chains = list(chains)
    while chains:
        alive = []
        for g in chains:
            try:
                next(g)
                alive.append(g)
            except StopIteration:
                pass
        chains = alive


def _proj_body(*refs, shift):
    if shift:
        x_ref, w_ref, wnext_ref, o_ref, wb_ref = refs
    else:
        x_ref, w_ref, o_ref, wb_ref = refs

    @pl.when(pl.program_id(1) == 0)
    def _():
        w = w_ref[...]
        if shift:
            w = jnp.concatenate([w[shift:, :], wnext_ref[:shift, :]], axis=0)
        wb_ref[...] = w.T.astype(_BF16)

    acc = _mm(x_ref[...], wb_ref[...])
    for j in range(o_ref.shape[0]):
        o_ref[j] = acc[:, j * _LANE:(j + 1) * _LANE].astype(o_ref.dtype)


def _project(xb, wt_all, layer, col0, n):
    m, k = xb.shape
    tm, tn = _tile(m, 2048), _tile(n, 512)
    shift = col0 % tn
    assert shift % _SUBLANE == 0
    blk0 = col0 // tn
    w_specs = [pl.BlockSpec((None, tn, k), lambda j, i: (layer, blk0 + j, 0))]
    if shift:
        w_specs.append(pl.BlockSpec((None, tn, k), lambda j, i: (layer, blk0 + j + 1, 0)))
    return pl.pallas_call(
        functools.partial(_proj_body, shift=shift),
        grid=(n // tn, m // tm),
        in_specs=[pl.BlockSpec((tm, k), lambda j, i: (i, 0))] + w_specs,
        out_specs=pl.BlockSpec((tn // _LANE, tm, _LANE), lambda j, i: (j, i, 0)),
        out_shape=jax.ShapeDtypeStruct((n // _LANE, m, _LANE), _BF16),
        scratch_shapes=[pltpu.VMEM((k, tn), _BF16)],
        compiler_params=_cparams(("parallel", "arbitrary")),
        name="in_proj",
    )(xb, *([wt_all] * len(w_specs)))


def _rowproj_body(w_ref, x_ref, o_ref):
    o_ref[...] = _mm_nt(w_ref[...], x_ref[...])


def _project_rows(xb, wt_rows, layer):
    m, k = xb.shape
    r = wt_rows.shape[1]
    tm = _tile(m, 1024)
    return pl.pallas_call(
        _rowproj_body,
        grid=(m // tm,),
        in_specs=[pl.BlockSpec((None, r, k), lambda i: (layer, 0, 0)),
                  pl.BlockSpec((tm, k), lambda i: (i, 0))],
        out_specs=pl.BlockSpec((r, tm), lambda i: (0, i)),
        out_shape=jax.ShapeDtypeStruct((r, m), _F32),
        compiler_params=_cparams(("parallel",)),
        name="gate_logits",
    )(wt_rows, xb)


def _dn_body(ab_ref, q_ref, k_ref, v_ref, z_ref, cq_ref, ck_ref, cv_ref, bl_ref, al_ref, nw_ref,
             o_ref, s_ref, tail_ref, xb_ref, *, hb, tl, dk):
    hg = pl.program_id(1)
    t = pl.program_id(2)
    c = _CHUNK
    nchunk = tl // c

    @pl.when(t == 0)
    def _():
        s_ref[...] = jnp.zeros_like(s_ref)
        tail_ref[...] = jnp.zeros_like(tail_ref)

    row = lax.broadcasted_iota(jnp.int32, (c, c), 0)
    col = lax.broadcasted_iota(jnp.int32, (c, c), 1)
    eye = row == col
    lower = row >= col
    eye_f = jnp.where(eye, 1.0, 0.0).astype(_F32)
    widths = []
    w_ = _INV_BASE
    while w_ < c:
        widths.append(w_)
        w_ *= 2
    same_blk = [(row // w_) == (col // w_) for w_ in widths]
    pos = lax.broadcasted_iota(jnp.int32, (_SUBLANE, tl), 1) % c

    def conv_silu(x_ref, c_ref, hh, idx):
        x = x_ref[hh].astype(_F32)
        xb_ref[idx, 0:_SUBLANE, :] = tail_ref[idx]
        xb_ref[idx, _SUBLANE:, :] = x
        tail_ref[idx] = x[tl - _SUBLANE:, :]
        w = c_ref[hh]
        y = w[_CONV_TAPS - 1:_CONV_TAPS, :] * x
        for j in range(_CONV_TAPS - 1):
            off = _SUBLANE - (_CONV_TAPS - 1) + j
            y = y + w[j:j + 1, :] * xb_ref[idx, off:off + tl, :]
        return y * _sigmoid(y)

    def l2n(x):
        return x * lax.rsqrt(jnp.sum(x * x, axis=-1, keepdims=True) + _RMS_EPS)

    def to_col(r):
        return jnp.sum(jnp.where(eye, r, 0.0), axis=-1, keepdims=True)

    heads = []
    for hh in range(hb):
        h = hg * hb + hh
        q = l2n(conv_silu(q_ref, cq_ref, hh, 3 * hh)) * (dk ** -0.5)
        k = l2n(conv_silu(k_ref, ck_ref, hh, 3 * hh + 1))
        v = conv_silu(v_ref, cv_ref, hh, 3 * hh + 2)
        beta_r = _sigmoid(bl_ref[hh])
        a_exp = jnp.exp(jnp.full((1, tl), ab_ref[0, h], _F32))
        g_r = -a_exp * _softplus(al_ref[hh] + ab_ref[1, h])
        g8 = jnp.broadcast_to(g_r, (_SUBLANE, tl))
        gc = g8
        sf = g8
        s_ = 1
        while s_ < c:
            gc = gc + jnp.where(pos >= s_, pltpu.roll(gc, s_, axis=1), 0.0)
            sf = sf + jnp.where(pos < c - s_, pltpu.roll(sf, tl - s_, axis=1), 0.0)
            s_ *= 2
        heads.append(dict(q=q, k=k, v=v, beta_r=beta_r, gc_r=gc[0:1, :],
                          tail_r=jnp.exp(sf[0:1, :] - g_r)))

    blocks = {}

    def chunk_chain(hh, ci):
        hd = heads[hh]
        lo, hi = ci * c, (ci + 1) * c
        qp, kp, vp = hd["q"][lo:hi], hd["k"][lo:hi], hd["v"][lo:hi]
        gc_rp = hd["gc_r"][:, lo:hi]
        beta_c, gc_c = to_col(hd["beta_r"][:, lo:hi]), to_col(gc_rp)
        egc_c = jnp.exp(gc_c)
        kpb = kp.astype(_BF16)
        qk = _mm_nt(jnp.concatenate([qp.astype(_BF16), kpb], axis=0), kpb)
        yield
        dm = jnp.exp(jnp.where(lower, gc_c - gc_rp, _NEG))
        a = jnp.where(eye, 0.0, qk[c:] * beta_c * dm)
        qd = (qk[:c] * dm).astype(_BF16)
        ad = jnp.where(same_blk[0], a, 0.0)
        x = eye_f - ad
        ap = ad.astype(_BF16)
        w_ = 2
        while w_ < _INV_BASE:
            ap = _mm(ap, ap).astype(_BF16)
            yield
            x = x + _mm(x.astype(_BF16), ap)
            yield
            w_ *= 2
        for lvl in range(len(widths)):
            inner = same_blk[lvl]
            off = (a if lvl + 1 == len(widths) else jnp.where(same_blk[lvl + 1], a, 0.0))
            off = jnp.where(inner, 0.0, off).astype(_BF16)
            xb = x.astype(_BF16)
            t1 = _mm(xb, off).astype(_BF16)
            yield
            x = x - _mm(t1, xb)
            yield
        rhs = jnp.concatenate([vp * beta_c, kp * (beta_c * egc_c)], axis=1).astype(_BF16)
        sol = _mm(x.astype(_BF16), rhs)
        yield
        blocks[(hh, ci)] = dict(
            u=sol[:, :dk], qd=qd,
            wq=jnp.concatenate([sol[:, dk:].astype(_BF16), (qp * egc_c).astype(_BF16)], axis=0),
            kt=(kp.T * hd["tail_r"][:, lo:hi]).astype(_BF16),
            gt=egc_c[c - 1:c, :])

    nw = nw_ref[...]

    def state_chain(hh):
        s = s_ref[hh]
        for ci in range(nchunk):
            b = blocks[(hh, ci)]
            r = _mm(b["wq"], s.astype(_BF16))
            yield
            vn = (b["u"] - r[:c]).astype(_BF16)
            s = s * b["gt"] + _mm(b["kt"], vn)
            yield
            o = r[c:] + _mm(b["qd"], vn)
            yield
            lo, hi = ci * c, (ci + 1) * c
            zf = z_ref[hh, lo:hi, :].astype(_F32)
            o = o * lax.rsqrt(jnp.mean(o * o, axis=-1, keepdims=True) + _RMS_EPS) * nw
            o_ref[lo:hi, hh * dk:(hh + 1) * dk] = (o * (zf * _sigmoid(zf))).astype(o_ref.dtype)
        s_ref[hh] = s

    _round_robin([chunk_chain(hh, ci) for ci in range(nchunk) for hh in range(hb)])
    _round_robin([state_chain(hh) for hh in range(hb)])


def _deltanet(hcm, rows, conv_t, ab, norm_w, *, batch, seq, heads, blk_q, blk_k, blk_v, blk_z):
    dk = hcm.shape[2]
    m = hcm.shape[1]
    hb = 4 if heads % 4 == 0 else 2
    tl = _tile(seq, 512)
    nt = seq // tl

    def act(off):
        return pl.BlockSpec((hb, tl, dk), lambda b, g, t: (off // hb + g, b * nt + t, 0))

    def taps(off):
        return pl.BlockSpec((hb, _CONV_TAPS, dk), lambda b, g, t: (off // hb + g, 0, 0))

    def logit(off):
        return pl.BlockSpec((hb, 1, tl), lambda b, g, t: (off // hb + g, 0, b * nt + t))

    return pl.pallas_call(
        functools.partial(_dn_body, hb=hb, tl=tl, dk=dk),
        grid=(batch, heads // hb, nt),
        in_specs=[pl.BlockSpec(memory_space=pltpu.SMEM),
                  act(blk_q), act(blk_k), act(blk_v), act(blk_z),
                  taps(0), taps(heads), taps(2 * heads),
                  logit(0), logit(heads),
                  pl.BlockSpec((1, dk), lambda b, g, t: (0, 0))],
        out_specs=pl.BlockSpec((tl, hb * dk), lambda b, g, t: (b * nt + t, g)),
        out_shape=jax.ShapeDtypeStruct((m, heads * dk), _BF16),
        scratch_shapes=[pltpu.VMEM((hb, dk, dk), _F32),
                        pltpu.VMEM((3 * hb, _SUBLANE, dk), _F32),
                        pltpu.VMEM((3 * hb, tl + _SUBLANE, dk), _F32)],
        compiler_params=_cparams(("parallel", "parallel", "arbitrary")),
        name="gated_deltanet",
    )(ab, hcm, hcm, hcm, hcm, conv_t, conv_t, conv_t, rows, rows, norm_w)


def _attn_body(li_ref, q_ref, k_ref, v_ref, lam_ref, nw_ref, o_ref, kx_ref, vx_ref, qs_ref, m_ref, acc_ref,
               s0_ref, s1_ref, cm_ref, p_ref, al_ref, *, heads, tq, d, seq):
    h = pl.program_id(1)
    hq = tq // 2
    tk = hq
    nq = seq // tq
    dv = 2 * d
    kw = 2 * dv
    lam_init = li_ref[0]

    def prepare_keys_values():
        hf = (h + 1).astype(_F32)
        slope2 = jnp.exp2(jnp.full((1, dv), -8.0 / heads, _F32) * hf) * _LOG2E
        rows = 4 * _ROW_CHUNK
        lane = lax.broadcasted_iota(jnp.int32, (rows, dv), 1)
        jloc = lax.broadcasted_iota(jnp.int32, (rows, dv), 0).astype(_F32)
        for r0 in range(0, seq, rows):
            rest = (jloc + float(r0 - seq)) * slope2
            ext = jnp.zeros((rows, dv), _F32)
            for piece in range(_BIAS_COLS):
                part = rest.astype(_BF16).astype(_F32)
                rest = rest - part
                ext = jnp.where(lane == piece, part, ext)
            kx_ref[r0:r0 + rows, dv:] = ext.astype(_BF16)
        kx_ref[:, :dv] = k_ref[0]
        vx_ref[:, :dv] = v_ref[0]
        vx_ref[:, dv:] = jnp.ones((seq, dv), _BF16)
        rpos = lax.broadcasted_iota(jnp.int32, (hq, tk), 0)
        cpos = lax.broadcasted_iota(jnp.int32, (hq, tk), 1)
        cm_ref[...] = jnp.where(rpos >= cpos, 0.0, _NEG).astype(_F32)

    def prepare_queries():
        lane = lax.broadcasted_iota(jnp.int32, (hq, kw), 1)
        ones = (lane >= dv) & (lane < dv + _BIAS_COLS)
        for g in range(2 * nq):
            qf = q_ref[0, g * hq:(g + 1) * hq, :].astype(_F32) * (d ** -0.5 * _LOG2E)
            qf = jnp.concatenate([qf, jnp.zeros((hq, dv), _F32)], axis=1)
            qs_ref[g * tq:g * tq + hq, :] = jnp.where(ones, 1.0, jnp.where(lane < d, qf, 0.0)).astype(_BF16)
            qs_ref[g * tq + hq:(g + 1) * tq, :] = jnp.where(
                ones, 1.0, jnp.where((lane >= d) & (lane < dv), qf, 0.0)).astype(_BF16)

    prepare_keys_values()
    prepare_queries()
    lp = lam_ref[...]
    lam = (jnp.exp(jnp.sum(lp[0:1] * lp[1:2], axis=-1, keepdims=True))
           - jnp.exp(jnp.sum(lp[2:3] * lp[3:4], axis=-1, keepdims=True)) + lam_init)
    nw = nw_ref[...]
    s_refs = (s0_ref, s1_ref)

    items = []
    for qi in range(nq):
        items += [(qi, 2 * qi, "diag0"), (qi, 2 * qi + 1, "diag1")]
        items += [(qi, t, "full") for t in range(2 * qi)]

    def logits(n):
        qi, t, kind = items[n]
        lo = tq if kind == "diag1" else 0
        q_rows = qs_ref[qi * 2 * tq + lo:(qi + 1) * 2 * tq, :]
        s_refs[n % 2][lo:, :] = _mm_nt(q_rows, kx_ref[t * tk:(t + 1) * tk, :])

    def absorb(n, lo, hi, causal):
        qi, t, _ = items[n]
        st, buf = qi % 2, n % 2
        for r0 in range(lo, hi, _ROW_CHUNK):
            rows = slice(r0, r0 + _ROW_CHUNK)
            s = s_refs[buf][rows, :]
            if causal:
                c0 = (r0 - lo) % hq
                s = s + cm_ref[c0:c0 + _ROW_CHUNK, :]
            m_old = m_ref[st, rows, :]
            m_new = jnp.maximum(m_old, jnp.max(s, axis=-1, keepdims=True))
            p_ref[buf, rows, :] = jnp.exp2(s - jnp.concatenate([m_new] * (tk // _LANE), axis=1)).astype(_BF16)
            al_ref[buf, rows, :] = jnp.exp2(m_old - m_new)
            m_ref[st, rows, :] = m_new
        alpha = al_ref[buf, lo:hi, :]
        acc_ref[st, lo:hi, :] = (jnp.concatenate([alpha, alpha], axis=1) * acc_ref[st, lo:hi, :]
                                 + _mm(p_ref[buf, lo:hi, :], vx_ref[t * tk:(t + 1) * tk, :]))

    def finish(qi):
        st = qi % 2
        for g in range(2):
            r0 = g * tq
            o1 = acc_ref[st, r0:r0 + hq, 0:dv] / acc_ref[st, r0:r0 + hq, dv:]
            o2 = acc_ref[st, r0 + hq:r0 + tq, 0:dv] / acc_ref[st, r0 + hq:r0 + tq, dv:]
            o = o1 - lam * o2
            o = o * lax.rsqrt(jnp.mean(o * o, axis=-1, keepdims=True) + _RMS_EPS) * nw
            o_ref[qi * tq + g * hq:qi * tq + (g + 1) * hq, :] = (o * (1.0 - lam_init)).astype(o_ref.dtype)

    logits(0)
    for n, (qi, t, kind) in enumerate(items):
        if kind == "diag0":
            m_ref[qi % 2] = jnp.full(m_ref.shape[1:], _NEG, _F32)
            acc_ref[qi % 2] = jnp.zeros(acc_ref.shape[1:], _F32)
        if n + 1 < len(items):
            logits(n + 1)
        if kind == "diag0":
            absorb(n, 0, tq, True)
            absorb(n, tq, 2 * tq, False)
        elif kind == "diag1":
            absorb(n, tq, 2 * tq, True)
        else:
            absorb(n, 0, 2 * tq, False)
        if n + 1 == len(items) or items[n + 1][0] != qi:
            finish(qi)


def _diff_attention(hcm, lam_params, norm_w, lam_init, *, batch, seq, heads, blk_q, blk_k, blk_v):
    dv = hcm.shape[2]
    d = dv // 2
    m = hcm.shape[1]
    tq = _tile(seq, 1024)
    return pl.pallas_call(
        functools.partial(_attn_body, heads=heads, tq=tq, d=d, seq=seq),
        grid=(batch, heads),
        in_specs=[pl.BlockSpec(memory_space=pltpu.SMEM),
                  pl.BlockSpec((1, seq, dv), lambda b, h: (blk_q + h, b, 0)),
                  pl.BlockSpec((1, seq, dv), lambda b, h: (blk_k + h, b, 0)),
                  pl.BlockSpec((1, seq, dv), lambda b, h: (blk_v + h, b, 0)),
                  pl.BlockSpec((4, d), lambda b, h: (0, 0)),
                  pl.BlockSpec((1, dv), lambda b, h: (0, 0))],
        out_specs=pl.BlockSpec((seq, dv), lambda b, h: (b, h)),
        out_shape=jax.ShapeDtypeStruct((m, heads * dv), _BF16),
        scratch_shapes=[pltpu.VMEM((seq, 2 * dv), _BF16),
                        pltpu.VMEM((seq, 2 * dv), _BF16),
                        pltpu.VMEM((2 * seq, 2 * dv), _BF16),
                        pltpu.VMEM((2, 2 * tq, _LANE), _F32),
                        pltpu.VMEM((2, 2 * tq, 2 * dv), _F32),
                        pltpu.VMEM((2 * tq, tq // 2), _F32),
                        pltpu.VMEM((2 * tq, tq // 2), _F32),
                        pltpu.VMEM((tq // 2, tq // 2), _F32),
                        pltpu.VMEM((2, 2 * tq, tq // 2), _BF16),
                        pltpu.VMEM((2, 2 * tq, _LANE), _F32)],
        compiler_params=_cparams(("parallel", "parallel")),
        name="diff_attention",
    )(lam_init, hcm, hcm, hcm, lam_params, norm_w)


def _merge_body(oa_ref, ob_ref, wa_ref, wb_ref, ga_ref, gb_ref, o_ref, wa16_ref, wb16_ref):
    @pl.when(pl.program_id(1) == 0)
    def _():
        wa16_ref[...] = wa_ref[...].astype(_BF16)
        wb16_ref[...] = wb_ref[...].astype(_BF16)

    ya = _mm(oa_ref[...], wa16_ref[...])
    yb = _mm(ob_ref[...], wb16_ref[...])
    for j in range(ga_ref.shape[0]):
        sl = slice(j * _LANE, (j + 1) * _LANE)
        o_ref[:, sl] = (_sigmoid(ga_ref[j].astype(_F32)) * ya[:, sl]
                        + _sigmoid(gb_ref[j].astype(_F32)) * yb[:, sl]).astype(o_ref.dtype)


def _merge(oa, ob, wa_all, wb_all, hcm, layer, *, blk_ga, blk_gb):
    m, k = oa.shape
    kb = ob.shape[1]
    n = wa_all.shape[2]
    tm, tn = _tile(m, 1024), _tile(n, 512)
    nb = tn // _LANE
    assert blk_ga % nb == 0 and blk_gb % nb == 0
    return pl.pallas_call(
        _merge_body,
        grid=(n // tn, m // tm),
        in_specs=[pl.BlockSpec((tm, k), lambda j, i: (i, 0)),
                  pl.BlockSpec((tm, kb), lambda j, i: (i, 0)),
                  pl.BlockSpec((None, k, tn), lambda j, i: (layer, 0, j)),
                  pl.BlockSpec((None, kb, tn), lambda j, i: (layer, 0, j)),
                  pl.BlockSpec((nb, tm, _LANE), lambda j, i: (blk_ga // nb + j, i, 0)),
                  pl.BlockSpec((nb, tm, _LANE), lambda j, i: (blk_gb // nb + j, i, 0))],
        out_specs=pl.BlockSpec((tm, tn), lambda j, i: (i, j)),
        out_shape=jax.ShapeDtypeStruct((m, n), _BF16),
        scratch_shapes=[pltpu.VMEM((k, tn), _BF16), pltpu.VMEM((kb, tn), _BF16)],
        compiler_params=_cparams(("parallel", "arbitrary")),
        name="branch_merge",
    )(oa, ob, wa_all, wb_all, hcm, hcm)


def _mm_res_ln_body(a_ref, w_ref, x_ref, g_ref, b_ref, of_ref, ob_ref, acc_ref, *, alpha, nk):
    kk = pl.program_id(1)

    half = a_ref.shape[0] // 2

    def finish(first):
        for r in (slice(0, half), slice(half, 2 * half)):
            acc = _mm(a_ref[r, :], w_ref[...])
            if not first:
                acc = acc + acc_ref[r, :]
            y = alpha * x_ref[r, :] + acc
            mu = jnp.mean(y, axis=-1, keepdims=True)
            yc = y - mu
            var = jnp.mean(yc * yc, axis=-1, keepdims=True)
            out = yc * lax.rsqrt(var + _LN_EPS) * g_ref[...] + b_ref[...]
            of_ref[r, :] = out
            ob_ref[r, :] = out.astype(ob_ref.dtype)

    if nk == 1:
        finish(True)
        return

    @pl.when(kk == 0)
    def _():
        acc_ref[...] = _mm(a_ref[...], w_ref[...])

    @pl.when((kk > 0) & (kk < nk - 1))
    def _():
        acc_ref[...] += _mm(a_ref[...], w_ref[...])

    @pl.when(kk == nk - 1)
    def _():
        finish(False)


def _mm_res_ln(a, w_all, x, ln_g, ln_b, layer, which, alpha, tk_pref):
    m, k = a.shape
    n = w_all.shape[2]
    tm, tk = _tile(m, 512), _tile(k, tk_pref)
    nk = k // tk
    return pl.pallas_call(
        functools.partial(_mm_res_ln_body, alpha=alpha, nk=nk),
        grid=(m // tm, nk),
        in_specs=[pl.BlockSpec((tm, tk), lambda i, j: (i, j)),
                  pl.BlockSpec((None, tk, n), lambda i, j: (layer, j, 0)),
                  pl.BlockSpec((tm, n), lambda i, j: (i, 0)),
                  pl.BlockSpec((None, 1, n), lambda i, j: (2 * layer + which, 0, 0)),
                  pl.BlockSpec((None, 1, n), lambda i, j: (2 * layer + which, 0, 0))],
        out_specs=[pl.BlockSpec((tm, n), lambda i, j: (i, 0)),
                   pl.BlockSpec((tm, n), lambda i, j: (i, 0))],
        out_shape=[jax.ShapeDtypeStruct((m, n), _F32), jax.ShapeDtypeStruct((m, n), _BF16)],
        scratch_shapes=[pltpu.VMEM((tm, n) if nk > 1 else (_SUBLANE, _LANE), _F32)],
        compiler_params=_cparams(("parallel", "arbitrary")),
        name="matmul_residual_layernorm",
    )(a, w_all, x, ln_g, ln_b)


def _swiglu_body(x_ref, wg_ref, wu_ref, o_ref, wg16_ref, wu16_ref):
    @pl.when(pl.program_id(1) == 0)
    def _():
        wg16_ref[...] = wg_ref[...].astype(_BF16)
        wu16_ref[...] = wu_ref[...].astype(_BF16)

    x = x_ref[...]
    g = _mm(x, wg16_ref[...])
    u = _mm(x, wu16_ref[...])
    o_ref[...] = (g * _sigmoid(g) * u).astype(o_ref.dtype)


def _swiglu(xb, wg_all, wu_all, layer):
    m, k = xb.shape
    n = wg_all.shape[2]
    tm, tn = _tile(m, 1024), _tile(n, 512)
    return pl.pallas_call(
        _swiglu_body,
        grid=(n // tn, m // tm),
        in_specs=[pl.BlockSpec((tm, k), lambda j, i: (i, 0)),
                  pl.BlockSpec((None, k, tn), lambda j, i: (layer, 0, j)),
                  pl.BlockSpec((None, k, tn), lambda j, i: (layer, 0, j))],
        out_specs=pl.BlockSpec((tm, tn), lambda j, i: (i, j)),
        out_shape=jax.ShapeDtypeStruct((m, n), _BF16),
        scratch_shapes=[pltpu.VMEM((k, tn), _BF16), pltpu.VMEM((k, tn), _BF16)],
        compiler_params=_cparams(("parallel", "arbitrary")),
        name="swiglu_gate_up",
    )(xb, wg_all, wu_all)


def kernel(x, w_in, conv_w, a_log, dt_bias, dn_norm_w, da_lambda, da_norm_w, w_branch_a, w_branch_b,
           w_out, ln_g, ln_b, w_gate, w_up, w_down):
    batch, seq, d_model = x.shape
    depth = w_in.shape[0]
    dn_heads = a_log.shape[1]
    dk = dn_norm_w.shape[1]
    d = da_lambda.shape[2]
    dv = da_norm_w.shape[1]
    da_heads = w_branch_b.shape[1] // dv
    dn_qk = dn_heads * dk
    da_qk = da_heads * 2 * d
    assert dk == _LANE and dv == _LANE and 2 * d == dv and dn_heads % 2 == 0
    assert w_in.shape[2] == 4 * dn_qk + 2 * dn_heads + 2 * da_qk + da_heads * dv + 2 * d_model
    m = batch * seq
    alpha = (2 * depth) ** 0.25

    small = slice(4 * dn_qk, 4 * dn_qk + 2 * dn_heads)
    n_rest = w_in.shape[2] - small.stop
    assert (2 * dn_heads) % _SUBLANE == 0
    w_in_t = jnp.swapaxes(w_in, 1, 2)
    w_rows = w_in_t[:, small, :].astype(_BF16)
    wo, wd = w_out.astype(_BF16), w_down.astype(_BF16)
    ln_g2 = ln_g.reshape(2 * depth, 1, d_model)
    ln_b2 = ln_b.reshape(2 * depth, 1, d_model)

    nb_dn, nb_da, nb_model = dn_qk // _LANE, da_qk // _LANE, d_model // _LANE
    blk_aq, blk_ak, blk_av = 0, nb_da, 2 * nb_da
    blk_ga = blk_av + da_heads
    blk_gb = blk_ga + nb_model

    xf = x.reshape(m, d_model)
    xb = xf.astype(_BF16)
    for layer in range(depth):
        h_dn = _project(xb, w_in_t, layer, 0, small.start)
        h_rest = _project(xb, w_in_t, layer, small.stop, n_rest)
        rows = _project_rows(xb, w_rows, layer).reshape(2 * dn_heads, 1, m)
        conv_t = jnp.swapaxes(conv_w[layer].reshape(_CONV_TAPS, 3 * dn_heads, dk), 0, 1)
        ab = jnp.stack([a_log[layer], dt_bias[layer]])
        o_a = _deltanet(h_dn, rows, conv_t, ab, dn_norm_w[layer].reshape(1, dk),
                        batch=batch, seq=seq, heads=dn_heads,
                        blk_q=0, blk_k=nb_dn, blk_v=2 * nb_dn, blk_z=3 * nb_dn)
        lam_init = jnp.full((1,), 0.8 - 0.6 * math.exp(-0.3 * layer), _F32)
        o_b = _diff_attention(h_rest, da_lambda[layer], da_norm_w[layer].reshape(1, dv), lam_init,
                              batch=batch, seq=seq, heads=da_heads,
                              blk_q=blk_aq, blk_k=blk_ak, blk_v=blk_av)
        merged = _merge(o_a, o_b, w_branch_a, w_branch_b, h_rest, layer, blk_ga=blk_ga, blk_gb=blk_gb)
        xf, xb = _mm_res_ln(merged, wo, xf, ln_g2, ln_b2, layer, 0, alpha, 2048)
        act = _swiglu(xb, w_gate, w_up, layer)
        xf, xb = _mm_res_ln(act, wd, xf, ln_g2, ln_b2, layer, 1, alpha, 1408)
    return xf.reshape(batch, seq, d_model)
```
